```python
import jax
import jax.numpy as jnp
from jax import lax
import numpy as np

D_MODEL = 4096
BATCH = 4
SEQ = 4096
DEPTH = 1

EPS = 1e-6
N_MEM = 256
D_MIX = D_MODEL
GLA_HEADS = 8
GLA_DV = D_MIX // 2
GLA_DK = GLA_DV // 2
GLA_HEAD_K = GLA_DK // GLA_HEADS
GLA_HEAD_V = GLA_DV // GLA_HEADS
GLA_GATE_RANK = 16
GLA_GATE_NORM = 16.0
GLA_CHUNK = 64
MOBA_WIDTH = D_MIX - GLA_DV
MOBA_HEAD_DIM = 128
MOBA_HEADS = MOBA_WIDTH // MOBA_HEAD_DIM
MOBA_BLOCK = 256
MOBA_TOPK = 3
MOBA_QCHUNK = 16
ROPE_THETA = 500000.0
ROPE_DIM = MOBA_HEAD_DIM // 4
CROSS_HEADS = 4
CROSS_HEAD_DIM = D_MODEL // CROSS_HEADS
D_FF = 4 * D_MODEL
IN_SIZES = (GLA_DK, GLA_DK, GLA_DV, GLA_GATE_RANK, GLA_DV, MOBA_WIDTH, MOBA_WIDTH, MOBA_WIDTH)
IN_WIDTH = 2 * GLA_DK + 2 * GLA_DV + GLA_GATE_RANK + 3 * MOBA_WIDTH

kernel_name = 'hybrid_gla_moba_block'


def rms_norm(x, g):
    xf = x.astype(jnp.float32)
    y = xf * lax.rsqrt(jnp.mean(xf * xf, axis=-1, keepdims=True) + EPS)
    return (y * g.astype(jnp.float32)).astype(x.dtype)


def partial_rotary(x, pos):
    half = ROPE_DIM // 2
    inv_freq = ROPE_THETA ** (-jnp.arange(0, ROPE_DIM, 2, dtype=jnp.float32) / ROPE_DIM)
    ang = pos.astype(jnp.float32)[:, None] * inv_freq[None, :]
    cos, sin = jnp.cos(ang), jnp.sin(ang)
    x1, x2, xp = x[..., :half], x[..., half:ROPE_DIM], x[..., ROPE_DIM:]
    rot = jnp.concatenate([x1 * cos - x2 * sin, x1 * sin + x2 * cos], axis=-1)
    return jnp.concatenate([rot.astype(x.dtype), xp], axis=-1)


def gla_chunked(q, k, v, g):
    B, H, S, dk = q.shape
    dv = v.shape[-1]
    C = GLA_CHUNK
    N = S // C
    f32 = jnp.float32
    out_dtype = v.dtype
    q = q.astype(f32) * (dk ** -0.5)
    k, v, g = k.astype(f32), v.astype(f32), g.astype(f32)
    q, k, g = (t.reshape(B, H, N, C, dk) for t in (q, k, g))
    v = v.reshape(B, H, N, C, dv)
    b = jnp.cumsum(g, axis=3)
    b_last = b[:, :, :, -1:, :]
    q_dec = q * jnp.exp(b)
    k_inv = k * jnp.exp(-b)
    k_end = k * jnp.exp(b_last - b)
    causal = jnp.tril(jnp.ones((C, C), dtype=bool))
    A = jnp.where(causal, jnp.einsum('bhnck,bhnsk->bhncs', q_dec, k_inv), 0.0)
    o_intra = jnp.einsum('bhncs,bhnsv->bhncv', A, v)
    dS = jnp.einsum('bhnsk,bhnsv->bhnkv', k_end, v)
    decay = jnp.exp(b_last[:, :, :, 0, :])

    def step(state, inp):
        d, ds = inp
        return state * d[..., None] + ds, state

    s0 = jnp.zeros((B, H, dk, dv), f32)
    _, s_before = lax.scan(step, s0, (jnp.moveaxis(decay, 2, 0), jnp.moveaxis(dS, 2, 0)))
    s_before = jnp.moveaxis(s_before, 0, 2)
    o_inter = jnp.einsum('bhnck,bhnkv->bhncv', q_dec, s_before)
    return (o_intra + o_inter).reshape(B, H, S, dv).astype(out_dtype)


def moba_attention(q, k, v):
    B, H, S, dh = q.shape
    L = MOBA_BLOCK
    NB = -(-S // L)
    pad = NB * L - S
    f32 = jnp.float32
    kp = jnp.pad(k, ((0, 0), (0, 0), (0, pad), (0, 0)))
    vp = jnp.pad(v, ((0, 0), (0, 0), (0, pad), (0, 0)))
    kb = kp.reshape(B, H, NB, L, dh)
    vb = vp.reshape(B, H, NB, L, dh)
    k_mean = jnp.mean(kb.astype(f32), axis=3)
    pos = jnp.arange(S)
    q_blk = pos // L
    gate = jnp.einsum('bhsd,bhnd->bhsn', q.astype(f32), k_mean)
    past = jnp.arange(NB)[None, :] < q_blk[:, None]
    gate = jnp.where(past, gate, -jnp.inf)
    topk = min(MOBA_TOPK, NB)
    _, sel = lax.top_k(gate, topk)
    sel_valid = sel < q_blk[:, None]

    QC = MOBA_QCHUNK
    NQ = S // QC
    qc = jnp.moveaxis(q.reshape(B, H, NQ, QC, dh), 2, 0)
    selc = jnp.moveaxis(sel.reshape(B, H, NQ, QC, topk), 2, 0)
    validc = jnp.moveaxis(sel_valid.reshape(B, H, NQ, QC, topk), 2, 0)
    starts = jnp.arange(NQ) * QC
    b_ix = jnp.arange(B)[:, None, None, None]
    h_ix = jnp.arange(H)[None, :, None, None]
    scale = dh ** -0.5

    def one_chunk(args):
        qq, ss, vv, start = args
        blk = start // L
        k_own = lax.dynamic_index_in_dim(kb, blk, axis=2, keepdims=False)
        v_own = lax.dynamic_index_in_dim(vb, blk, axis=2, keepdims=False)
        q_pos = start + jnp.arange(QC)
        k_pos = blk * L + jnp.arange(L)
        s_own = jnp.einsum('bhqd,bhkd->bhqk', qq, k_own).astype(f32) * scale
        s_own = jnp.where(k_pos[None, :] <= q_pos[:, None], s_own, -jnp.inf)
        k_sel = kb[b_ix, h_ix, ss]
        v_sel = vb[b_ix, h_ix, ss]
        s_sel = jnp.einsum('bhqd,bhqjkd->bhqjk', qq, k_sel).astype(f32) * scale
        s_sel = jnp.where(vv[..., None], s_sel, -jnp.inf).reshape(B, H, QC, topk * L)
        p = jax.nn.softmax(jnp.concatenate([s_own, s_sel], axis=-1), axis=-1)
        p_own = p[..., :L].astype(v.dtype)
        p_sel = p[..., L:].reshape(B, H, QC, topk, L).astype(v.dtype)
        return (jnp.einsum('bhqk,bhkd->bhqd', p_own, v_own)
                + jnp.einsum('bhqjk,bhqjkd->bhqd', p_sel, v_sel))

    o = lax.map(one_chunk, (qc, selc, validc, starts))
    return jnp.moveaxis(o, 0, 2).reshape(B, H, S, dh)


def hybrid_mixer(xn, w_in, w_gate_up, b_gate, gla_norm_g, w_out):
    B, S, _ = xn.shape
    proj = xn @ w_in
    gq, gk, gv, g_low, g_out, mq, mk, mv = jnp.split(
        proj, np.cumsum(IN_SIZES)[:-1].tolist(), axis=-1)

    def heads(t, n):
        return t.reshape(B, S, n, -1).transpose(0, 2, 1, 3)

    g_log = jax.nn.log_sigmoid((g_low @ w_gate_up + b_gate).astype(jnp.float32)) / GLA_GATE_NORM
    o_gla = gla_chunked(heads(gq, GLA_HEADS), heads(gk, GLA_HEADS),
                        heads(gv, GLA_HEADS), heads(g_log, GLA_HEADS))
    o_gla = rms_norm(o_gla.transpose(0, 2, 1, 3), gla_norm_g)
    o_gla = (o_gla * jax.nn.silu(g_out.reshape(B, S, GLA_HEADS, GLA_HEAD_V))).reshape(B, S, GLA_DV)

    pos = jnp.arange(S)
    q = partial_rotary(heads(mq, MOBA_HEADS), pos)
    k = partial_rotary(heads(mk, MOBA_HEADS), pos)
    o_moba = moba_attention(q, k, heads(mv, MOBA_HEADS))
    o_moba = o_moba.transpose(0, 2, 1, 3).reshape(B, S, MOBA_WIDTH)

    return jnp.concatenate([o_gla, o_moba], axis=-1) @ w_out


def cross_attention(hn, mem_n, w_cq, w_ck, w_cv, w_co):
    B, S, D = hn.shape
    M = mem_n.shape[1]
    q = (hn @ w_cq).reshape(B, S, CROSS_HEADS, CROSS_HEAD_DIM)
    k = (mem_n @ w_ck).reshape(B, M, CROSS_HEADS, CROSS_HEAD_DIM)
    v = (mem_n @ w_cv).reshape(B, M, CROSS_HEADS, CROSS_HEAD_DIM)
    s = jnp.einsum('bshd,bmhd->bhsm', q, k).astype(jnp.float32) * (CROSS_HEAD_DIM ** -0.5)
    p = jax.nn.softmax(s, axis=-1).astype(v.dtype)
    o = jnp.einsum('bhsm,bmhd->bshd', p, v).reshape(B, S, D)
    return o @ w_co


def squared_relu_mlp(hn, w_up, w_down):
    return jnp.square(jax.nn.relu(hn @ w_up)) @ w_down


def setup_inputs(seed: int = 0) -> dict:
    key = jax.random.key(seed)
    ks = jax.random.split(key, 20)
    f32 = jnp.float32

    def w(k, shape, fan_in):
        return jax.random.normal(k, shape, f32) * (fan_in ** -0.5)

    def gain(k, shape):
        return 1.0 + 0.02 * jax.random.normal(k, shape, f32)

    L = DEPTH
    return {
        'x': jax.random.normal(ks[0], (BATCH, SEQ, D_MODEL), f32),
        'mem': jax.random.normal(ks[1], (BATCH, N_MEM, D_MODEL), f32),
        'norm_mix_g': gain(ks[2], (L, D_MODEL)),
        'w_in': w(ks[3], (L, D_MODEL, IN_WIDTH), D_MODEL),
        'w_gate_up': w(ks[4], (L, GLA_GATE_RANK, GLA_DK), GLA_GATE_RANK),
        'b_gate': 0.1 * jax.random.normal(ks[5], (L, GLA_DK), f32),
        'gla_norm_g': gain(ks[6], (L, GLA_HEAD_V)),
        'w_out': w(ks[7], (L, D_MIX, D_MODEL), D_MIX),
        'norm_cross_g': gain(ks[8], (L, D_MODEL)),
        'norm_mem_g': gain(ks[9], (L, D_MODEL)),
        'w_cq': w(ks[10], (L, D_MODEL, D_MODEL), D_MODEL),
        'w_ck': w(ks[11], (L, D_MODEL, D_MODEL), D_MODEL),
        'w_cv': w(ks[12], (L, D_MODEL, D_MODEL), D_MODEL),
        'w_co': w(ks[13], (L, D_MODEL, D_MODEL), D_MODEL),
        'norm_mlp_g': gain(ks[14], (L, D_MODEL)),
        'w_up': w(ks[15], (L, D_MODEL, D_FF), D_MODEL),
        'w_down': w(ks[16], (L, D_FF, D_MODEL), D_FF),
        'norm_final_g': gain(ks[17], (D_MODEL,)),
    }


def reference(x, mem, norm_mix_g, w_in, w_gate_up, b_gate, gla_norm_g, w_out,
              norm_cross_g, norm_mem_g, w_cq, w_ck, w_cv, w_co,
              norm_mlp_g, w_up, w_down, norm_final_g):
    h = x
    for l in range(DEPTH):
        h = h + hybrid_mixer(rms_norm(h, norm_mix_g[l]), w_in[l], w_gate_up[l],
                             b_gate[l], gla_norm_g[l], w_out[l])
        h = h + cross_attention(rms_norm(h, norm_cross_g[l]), rms_norm(mem, norm_mem_g[l]),
                                w_cq[l], w_ck[l], w_cv[l], w_co[l])
        h = h + squared_relu_mlp(rms_norm(h, norm_mlp_g[l]), w_up[l], w_down[l])
    return rms_norm(h, norm_final_g)
```

```python
import functools

import jax
import jax.numpy as jnp
from jax import lax
from jax.experimental import pallas as pl
from jax.experimental.pallas import tpu as pltpu

F32 = jnp.float32
BF16 = jnp.bfloat16
HIGHEST = lax.Precision.HIGHEST

EPS = 1e-6
LANES = 128
VMEM_LIMIT_BYTES = 56 * 1024 * 1024

GLA_HEADS = 8
GLA_HEAD_K = 128
GLA_HEAD_V = 256
GLA_GATE_RANK = 16
GLA_GATE_NORM = 16.0
GLA_CHUNK = 64
MOBA_HEAD_DIM = 128
MOBA_BLOCK = 256
MOBA_TOPK = 3
ROPE_THETA = 500000.0
ROPE_DIM = MOBA_HEAD_DIM // 4
ROPE_HALF = ROPE_DIM // 2
CROSS_HEADS = 4
MASK_VALUE = -1e30
LOG2E = 1.4426950408889634


def _params(*sem):
    return pltpu.CompilerParams(dimension_semantics=sem, vmem_limit_bytes=VMEM_LIMIT_BYTES)


def _rmsnorm_body(x_ref, g_ref, o_ref):
    x = x_ref[...].astype(F32)
    ms = jnp.mean(x * x, axis=-1, keepdims=True)
    y = x * lax.rsqrt(ms + EPS)
    o_ref[...] = (y * g_ref[...]).astype(o_ref.dtype)


def _rmsnorm(x, g, out_dtype, rows=256):
    m, d = x.shape
    rows = min(rows, m)
    return pl.pallas_call(
        _rmsnorm_body,
        grid=(m // rows,),
        in_specs=[pl.BlockSpec((rows, d), lambda i: (i, 0)),
                  pl.BlockSpec((1, d), lambda i: (0, 0))],
        out_specs=pl.BlockSpec((rows, d), lambda i: (i, 0)),
        out_shape=jax.ShapeDtypeStruct((m, d), out_dtype),
        compiler_params=_params("parallel"),
        name="rmsnorm",
    )(x, g.reshape(1, d).astype(F32))


def _mm_body(*refs, nk, act, has_res):
    if has_res:
        a_ref, w_ref, r_ref, o_ref = refs[:4]
        scratch = refs[4:]
    else:
        a_ref, w_ref, o_ref = refs[:3]
        r_ref = None
        scratch = refs[3:]

    def finish(acc):
        if act == "relu2":
            acc = jnp.square(jnp.maximum(acc, 0.0))
        if has_res:
            acc = acc + r_ref[...]
        o_ref[...] = acc.astype(o_ref.dtype)

    if nk == 1:
        finish(jnp.dot(a_ref[...], w_ref[...], preferred_element_type=F32))
        return

    acc_ref, = scratch
    k = pl.program_id(2)

    @pl.when(k == 0)
    def _():
        acc_ref[...] = jnp.dot(a_ref[...], w_ref[...], preferred_element_type=F32)

    @pl.when(k > 0)
    def _():
        acc_ref[...] += jnp.dot(a_ref[...], w_ref[...], preferred_element_type=F32)

    @pl.when(k == nk - 1)
    def _():
        finish(acc_ref[...])


def _matmul(a, w, *, out_dtype, res=None, act=None, tm=1024, tn=1024, tk=4096, name="matmul"):
    m, kdim = a.shape
    n = w.shape[1]
    tm, tn, tk = min(tm, m), min(tn, n), min(tk, kdim)
    nk = kdim // tk
    in_specs = [pl.BlockSpec((tm, tk), lambda i, j, k: (i, k)),
                pl.BlockSpec((tk, tn), lambda i, j, k: (k, j))]
    args = [a, w]
    if res is not None:
        in_specs.append(pl.BlockSpec((tm, tn), lambda i, j, k: (i, j)))
        args.append(res)
    scratch = [pltpu.VMEM((tm, tn), F32)] if nk > 1 else []
    return pl.pallas_call(
        functools.partial(_mm_body, nk=nk, act=act, has_res=res is not None),
        grid=(m // tm, n // tn, nk),
        in_specs=in_specs,
        out_specs=pl.BlockSpec((tm, tn), lambda i, j, k: (i, j)),
        out_shape=jax.ShapeDtypeStruct((m, n), out_dtype),
        scratch_shapes=scratch,
        compiler_params=_params("parallel", "parallel", "arbitrary"),
        name=name,
    )(*args)


def _mm_rot_body(a_ref, w_ref, cos_ref, sin_ref, o_ref, *, scale):
    acc = jnp.dot(a_ref[...], w_ref[...], preferred_element_type=F32)
    tm, tn = acc.shape
    cos = cos_ref[...]
    sin = sin_ref[...]
    lane = lax.broadcasted_iota(jnp.int32, (tm, LANES), 1)
    for c in range(tn // LANES):
        xs = acc[:, c * LANES:(c + 1) * LANES]
        partner = jnp.where(lane < ROPE_HALF,
                            pltpu.roll(xs, LANES - ROPE_HALF, 1),
                            pltpu.roll(xs, ROPE_HALF, 1))
        y = xs * cos + partner * sin
        if scale != 1.0:
            y = y * scale
        o_ref[:, c * LANES:(c + 1) * LANES] = y.astype(o_ref.dtype)


def _matmul_rotary(a, w, cos_t, sin_t, *, scale, tm=1024, tn=1024, name="proj_rot"):
    m, kdim = a.shape
    n = w.shape[1]
    seq = cos_t.shape[0]
    tm, tn = min(tm, m, seq), min(tn, n)
    tiles_per_seq = seq // tm
    return pl.pallas_call(
        functools.partial(_mm_rot_body, scale=scale),
        grid=(m // tm, n // tn),
        in_specs=[pl.BlockSpec((tm, kdim), lambda i, j: (i, 0)),
                  pl.BlockSpec((kdim, tn), lambda i, j: (0, j)),
                  pl.BlockSpec((tm, LANES), lambda i, j: (i % tiles_per_seq, 0)),
                  pl.BlockSpec((tm, LANES), lambda i, j: (i % tiles_per_seq, 0))],
        out_specs=pl.BlockSpec((tm, tn), lambda i, j: (i, j)),
        out_shape=jax.ShapeDtypeStruct((m, n), BF16),
        compiler_params=_params("parallel", "parallel"),
        name=name,
    )(a, w, cos_t, sin_t)


def _rotary_tables(seq):
    inv_freq = ROPE_THETA ** (-jnp.arange(0, ROPE_DIM, 2, dtype=F32) / ROPE_DIM)
    ang = jnp.arange(seq).astype(F32)[:, None] * inv_freq[None, :]
    cos, sin = jnp.cos(ang), jnp.sin(ang)
    pad = MOBA_HEAD_DIM - ROPE_DIM
    cos_t = jnp.concatenate([cos, cos, jnp.ones((seq, pad), F32)], axis=-1)
    sin_t = jnp.concatenate([-sin, sin, jnp.zeros((seq, pad), F32)], axis=-1)
    return cos_t, sin_t


def _gate_body(gl_ref, w_ref, b_ref, o_ref):
    z = jnp.dot(gl_ref[...], w_ref[...], preferred_element_type=F32, precision=HIGHEST)
    z = z + b_ref[...]
    log_sig = jnp.minimum(z, 0.0) - jnp.log1p(jnp.exp(-jnp.abs(z)))
    o_ref[...] = log_sig * (1.0 / GLA_GATE_NORM)


def _gate(g_low, w_pad, b, rows=1024):
    m, r = g_low.shape
    n = w_pad.shape[1]
    rows = min(rows, m)
    return pl.pallas_call(
        _gate_body,
        grid=(m // rows,),
        in_specs=[pl.BlockSpec((rows, r), lambda i: (i, 0)),
                  pl.BlockSpec((r, n), lambda i: (0, 0)),
                  pl.BlockSpec((1, n), lambda i: (0, 0))],
        out_specs=pl.BlockSpec((rows, n), lambda i: (i, 0)),
        out_shape=jax.ShapeDtypeStruct((m, n), F32),
        compiler_params=_params("parallel"),
        name="gla_gate",
    )(g_low, w_pad, b.reshape(1, n).astype(F32))


def _gla_body(q_ref, k_ref, v_ref, g_ref, go_ref, gn_ref, o_ref, st_ref, *, chunks):
    c_len = GLA_CHUNK

    @pl.when(pl.program_id(2) == 0)
    def _():
        st_ref[...] = jnp.zeros_like(st_ref)

    row = lax.broadcasted_iota(jnp.int32, (c_len, c_len), 0)
    col = lax.broadcasted_iota(jnp.int32, (c_len, c_len), 1)
    causal = col <= row
    tri = causal.astype(F32)
    q_scale = GLA_HEAD_K ** -0.5
    gn = gn_ref[...]

    for c in range(chunks):
        sl = pl.ds(c * c_len, c_len)
        g = g_ref[sl, :]
        b = jnp.dot(tri, g, preferred_element_type=F32, precision=HIGHEST)
        b_last = b[c_len - 1:c_len, :]
        q = q_ref[sl, :].astype(F32) * q_scale
        k = k_ref[sl, :].astype(F32)
        v = v_ref[sl, :]
        q_dec = (q * jnp.exp(b)).astype(BF16)
        k_inv = (k * jnp.exp(-b)).astype(BF16)
        k_end = (k * jnp.exp(b_last - b)).astype(BF16)
        a = lax.dot_general(q_dec, k_inv, (((1,), (1,)), ((), ())), preferred_element_type=F32)
        a = jnp.where(causal, a, 0.0).astype(BF16)
        st = st_ref[...]
        o = jnp.dot(a, v, preferred_element_type=F32)
        o = o + lax.dot_general(q_dec, st.astype(BF16), (((1,), (1,)), ((), ())),
                                preferred_element_type=F32)
        d_st = lax.dot_general(v, k_end, (((0,), (0,)), ((), ())), preferred_element_type=F32)
        st_ref[...] = st * jnp.exp(b_last) + d_st

        ms = jnp.mean(o * o, axis=-1, keepdims=True)
        y = o * lax.rsqrt(ms + EPS) * gn
        go = go_ref[sl, :].astype(F32)
        y = y * (go * jax.nn.sigmoid(go))
        o_ref[sl, :] = y.astype(o_ref.dtype)


def _gla(qkv, g, g_out, gn, *, batch, seq, heads, rows=256):
    m = batch * seq
    rows = min(rows, seq)
    steps = seq // rows
    hk, hv = GLA_HEAD_K, GLA_HEAD_V
    k_off = heads
    v_off = heads * 2 * hk // hv
    row_map = lambda b, h, t: b * steps + t
    return pl.pallas_call(
        functools.partial(_gla_body, chunks=rows // GLA_CHUNK),
        grid=(batch, heads, steps),
        in_specs=[pl.BlockSpec((rows, hk), lambda b, h, t: (row_map(b, h, t), h)),
                  pl.BlockSpec((rows, hk), lambda b, h, t: (row_map(b, h, t), k_off + h)),
                  pl.BlockSpec((rows, hv), lambda b, h, t: (row_map(b, h, t), v_off + h)),
                  pl.BlockSpec((rows, hk), lambda b, h, t: (row_map(b, h, t), h)),
                  pl.BlockSpec((rows, hv), lambda b, h, t: (row_map(b, h, t), h)),
                  pl.BlockSpec((1, hv), lambda b, h, t: (0, 0))],
        out_specs=pl.BlockSpec((rows, hv), lambda b, h, t: (row_map(b, h, t), h)),
        out_shape=jax.ShapeDtypeStruct((m, heads * hv), BF16),
        scratch_shapes=[pltpu.VMEM((hv, hk), F32)],
        compiler_params=_params("parallel", "parallel", "arbitrary"),
        name="gla",
    )(qkv, qkv, qkv, g, g_out, gn.reshape(1, hv).astype(F32))


def _moba_body(q_ref, k_ref, v_ref, o_ref, kaug_ref, kmh_ref, kml_ref, *, n_blocks):
    blk = MOBA_BLOCK
    dh = MOBA_HEAD_DIM
    seq = n_blocks * blk
    qb = pl.program_id(2)

    @pl.when(qb == 0)
    def _():
        kaug_ref[:, :dh] = k_ref[...]
        r = lax.broadcasted_iota(jnp.int32, (seq, dh), 0)
        l = lax.broadcasted_iota(jnp.int32, (seq, dh), 1)
        kaug_ref[:, dh:] = (l * blk <= r).astype(BF16) * (r < (l + 1) * blk).astype(BF16)
        j = lax.broadcasted_iota(jnp.int32, (dh, seq), 0)
        s = lax.broadcasted_iota(jnp.int32, (dh, seq), 1)
        avg = jnp.where((j * blk <= s) & (s < (j + 1) * blk), 1.0 / blk, 0.0).astype(BF16)
        km = jnp.dot(avg, k_ref[...], preferred_element_type=F32)
        hi = km.astype(BF16)
        kmh_ref[...] = hi
        kml_ref[...] = (km - hi.astype(F32)).astype(BF16)

    q = q_ref[...]
    nt = (((1,), (1,)), ((), ()))
    gate = (lax.dot_general(q, kmh_ref[...], nt, preferred_element_type=F32)
            + lax.dot_general(q, kml_ref[...], nt, preferred_element_type=F32))
    col = lax.broadcasted_iota(jnp.int32, (blk, dh), 1)
    past = col < qb
    gm = jnp.where(past, gate, -jnp.inf)
    rank = jnp.zeros((blk, dh), jnp.int32)
    for j in range(n_blocks - 1):
        gj = gm[:, j:j + 1]
        beats = (gj > gm) | ((gj == gm) & (col > j))
        rank = rank + beats.astype(jnp.int32)
    sel = past & (rank < MOBA_TOPK)
    bias = jnp.where(sel | (col >= n_blocks), 0.0, MASK_VALUE).astype(BF16)
    q_aug = jnp.concatenate([q, bias], axis=1)

    start = pl.multiple_of(qb * blk, blk)
    k_own = k_ref[pl.ds(start, blk), :]
    v_own = v_ref[pl.ds(start, blk), :]
    s = lax.dot_general(q, k_own, nt, preferred_element_type=F32)
    qi = lax.broadcasted_iota(jnp.int32, (blk, blk), 0)
    ki = lax.broadcasted_iota(jnp.int32, (blk, blk), 1)
    s = jnp.where(ki <= qi, s, MASK_VALUE)
    m0 = jnp.max(s, axis=-1, keepdims=True)
    p = jnp.exp2(s - m0)
    l0 = jnp.sum(p, axis=-1, keepdims=True)
    acc0 = jnp.dot(p.astype(BF16), v_own, preferred_element_type=F32)

    def step(kb, carry):
        m_i, l_i, acc = carry
        off = pl.multiple_of(kb * blk, blk)
        k_blk = kaug_ref[pl.ds(off, blk), :]
        v_blk = v_ref[pl.ds(off, blk), :]
        s = lax.dot_general(q_aug, k_blk, nt, preferred_element_type=F32)
        m_new = jnp.maximum(m_i, jnp.max(s, axis=-1, keepdims=True))
        alpha = jnp.exp2(m_i - m_new)
        p = jnp.exp2(s - m_new)
        l_new = alpha * l_i + jnp.sum(p, axis=-1, keepdims=True)
        acc = alpha * acc + jnp.dot(p.astype(BF16), v_blk, preferred_element_type=F32)
        return m_new, l_new, acc

    _, l_f, acc_f = lax.fori_loop(0, qb, step, (m0, l0, acc0))
    o_ref[...] = (acc_f / l_f).astype(o_ref.dtype)


def _moba(q, k, v, *, batch, seq, heads):
    m = batch * seq
    blk, dh = MOBA_BLOCK, MOBA_HEAD_DIM
    n_blocks = seq // blk
    return pl.pallas_call(
        functools.partial(_moba_body, n_blocks=n_blocks),
        grid=(batch, heads, n_blocks),
        in_specs=[pl.BlockSpec((blk, dh), lambda b, h, t: (b * n_blocks + t, h)),
                  pl.BlockSpec((seq, dh), lambda b, h, t: (b, h)),
                  pl.BlockSpec((seq, dh), lambda b, h, t: (b, h))],
        out_specs=pl.BlockSpec((blk, dh), lambda b, h, t: (b * n_blocks + t, h)),
        out_shape=jax.ShapeDtypeStruct((m, heads * dh), BF16),
        scratch_shapes=[pltpu.VMEM((seq, 2 * dh), BF16),
                        pltpu.VMEM((dh, dh), BF16),
                        pltpu.VMEM((dh, dh), BF16)],
        compiler_params=_params("parallel", "parallel", "arbitrary"),
        name="moba",
    )(q, k, v)


def _mix_out_body(a1_ref, a2_ref, w_ref, r_ref, o_ref, acc_ref):
    k = pl.program_id(2)

    @pl.when(k == 0)
    def _():
        acc_ref[...] = jnp.dot(a1_ref[...], w_ref[...], preferred_element_type=F32)

    @pl.when(k == 1)
    def _():
        acc = acc_ref[...] + jnp.dot(a2_ref[...], w_ref[...], preferred_element_type=F32)
        o_ref[...] = acc + r_ref[...]


def _mix_out(a1, a2, w, res, tm=1024, tn=1024):
    m, half = a1.shape
    n = w.shape[1]
    tm, tn = min(tm, m), min(tn, n)
    return pl.pallas_call(
        _mix_out_body,
        grid=(m // tm, n // tn, 2),
        in_specs=[pl.BlockSpec((tm, half), lambda i, j, k: (i, 0)),
                  pl.BlockSpec((tm, half), lambda i, j, k: (i, 0)),
                  pl.BlockSpec((half, tn), lambda i, j, k: (k, j)),
                  pl.BlockSpec((tm, tn), lambda i, j, k: (i, j))],
        out_specs=pl.BlockSpec((tm, tn), lambda i, j, k: (i, j)),
        out_shape=jax.ShapeDtypeStruct((m, n), F32),
        scratch_shapes=[pltpu.VMEM((tm, tn), F32)],
        compiler_params=_params("parallel", "parallel", "arbitrary"),
        name="mix_out",
    )(a1, a2, w, res)


def _cross_body(q_ref, k_ref, v_ref, o_ref, *, heads):
    d = q_ref.shape[1]
    dh = d // heads
    scale = dh ** -0.5
    for h in range(heads):
        sl = slice(h * dh, (h + 1) * dh)
        s = lax.dot_general(q_ref[:, sl], k_ref[:, sl], (((1,), (1,)), ((), ())),
                            preferred_element_type=F32) * scale
        s = s - jnp.max(s, axis=-1, keepdims=True)
        e = jnp.exp(s)
        p = e / jnp.sum(e, axis=-1, keepdims=True)
        o_ref[:, sl] = jnp.dot(p.astype(BF16), v_ref[:, sl],
                               preferred_element_type=F32).astype(o_ref.dtype)


def _cross(q, k, v, *, batch, seq, n_mem, heads, rows=512):
    m, d = q.shape
    rows = min(rows, seq)
    steps = seq // rows
    return pl.pallas_call(
        functools.partial(_cross_body, heads=heads),
        grid=(batch, steps),
        in_specs=[pl.BlockSpec((rows, d), lambda b, t: (b * steps + t, 0)),
                  pl.BlockSpec((n_mem, d), lambda b, t: (b, 0)),
                  pl.BlockSpec((n_mem, d), lambda b, t: (b, 0))],
        out_specs=pl.BlockSpec((rows, d), lambda b, t: (b * steps + t, 0)),
        out_shape=jax.ShapeDtypeStruct((m, d), BF16),
        compiler_params=_params("parallel", "parallel"),
        name="cross_attn",
    )(q, k, v)


def _layer(h, mem2, batch, seq, n_mem, norm_mix_g, w_in, w_gate_up, b_gate, gla_norm_g, w_out,
           norm_cross_g, norm_mem_g, w_cq, w_ck, w_cv, w_co, norm_mlp_g, w_up, w_down):
    d = h.shape[1]
    gla_dk = GLA_HEADS * GLA_HEAD_K
    gla_dv = GLA_HEADS * GLA_HEAD_V
    moba_w = d - gla_dv
    moba_heads = moba_w // MOBA_HEAD_DIM
    o_glow = 2 * gla_dk + gla_dv
    o_gout = o_glow + GLA_GATE_RANK
    o_mq = o_gout + gla_dv
    o_mk = o_mq + moba_w
    o_mv = o_mk + moba_w
    w_gla = w_in[:, :o_glow].astype(BF16)
    w_glow = jnp.pad(w_in[:, o_glow:o_gout], ((0, 0), (0, LANES - GLA_GATE_RANK))).astype(BF16)
    w_gout = w_in[:, o_gout:o_mq].astype(BF16)
    w_mq = w_in[:, o_mq:o_mk].astype(BF16)
    w_mk = w_in[:, o_mk:o_mv].astype(BF16)
    w_mv = w_in[:, o_mv:].astype(BF16)
    w_gu = jnp.pad(w_gate_up, ((0, LANES - GLA_GATE_RANK), (0, 0)))

    xn = _rmsnorm(h, norm_mix_g, BF16)
    gla_qkv = _matmul(xn, w_gla, out_dtype=BF16, name="proj_gla_qkv")
    g_low = _matmul(xn, w_glow, out_dtype=F32, name="proj_gla_glow")
    g_out = _matmul(xn, w_gout, out_dtype=BF16, name="proj_gla_gout")
    g = _gate(g_low, w_gu, b_gate)
    o_gla = _gla(gla_qkv, g, g_out, gla_norm_g, batch=batch, seq=seq, heads=GLA_HEADS)

    cos_t, sin_t = _rotary_tables(seq)
    mq = _matmul_rotary(xn, w_mq, cos_t, sin_t, scale=MOBA_HEAD_DIM ** -0.5 * LOG2E, name="proj_moba_q")
    mk = _matmul_rotary(xn, w_mk, cos_t, sin_t, scale=1.0, name="proj_moba_k")
    mv = _matmul(xn, w_mv, out_dtype=BF16, name="proj_moba_v")
    o_moba = _moba(mq, mk, mv, batch=batch, seq=seq, heads=moba_heads)

    h = _mix_out(o_gla, o_moba, w_out.astype(BF16), h)

    hn = _rmsnorm(h, norm_cross_g, BF16)
    mem_n = _rmsnorm(mem2, norm_mem_g, BF16)
    cq = _matmul(hn, w_cq.astype(BF16), out_dtype=BF16, name="proj_cross_q")
    ck = _matmul(mem_n, w_ck.astype(BF16), out_dtype=BF16, name="proj_cross_k")
    cv = _matmul(mem_n, w_cv.astype(BF16), out_dtype=BF16, name="proj_cross_v")
    o_cross = _cross(cq, ck, cv, batch=batch, seq=seq, n_mem=n_mem, heads=CROSS_HEADS)
    h = _matmul(o_cross, w_co.astype(BF16), out_dtype=F32, res=h, name="proj_cross_out")

    hn = _rmsnorm(h, norm_mlp_g, BF16)
    u = _matmul(hn, w_up.astype(BF16), out_dtype=BF16, act="relu2", name="mlp_up")
    h = _matmul(u, w_down.astype(BF16), out_dtype=F32, res=h, tk=2048, name="mlp_down")
    return h


def kernel(x, mem, norm_mix_g, w_in, w_gate_up, b_gate, gla_norm_g, w_out, norm_cross_g, norm_mem_g,
           w_cq, w_ck, w_cv, w_co, norm_mlp_g, w_up, w_down, norm_final_g):
    batch, seq, d = x.shape
    n_mem = mem.shape[1]
    h = x.reshape(batch * seq, d)
    mem2 = mem.reshape(batch * n_mem, d)
    for l in range(norm_mix_g.shape[0]):
        h = _layer(h, mem2, batch, seq, n_mem, norm_mix_g[l], w_in[l], w_gate_up[l], b_gate[l],
                   gla_norm_g[l], w_out[l], norm_cross_g[l], norm_mem_g[l], w_cq[l], w_ck[l],
                   w_cv[l], w_co[l], norm_mlp_g[l], w_up[l], w_down[l])
    out = _rmsnorm(h, norm_final_g, x.dtype)
    return out.reshape(batch, seq, d)
```

```python
import functools

import jax
import jax.numpy as jnp
from jax import lax
from jax.experimental import pallas as pl
from jax.experimental.pallas import tpu as pltpu

F32 = jnp.float32
BF16 = jnp.bfloat16
HIGHEST = lax.Precision.HIGHEST

EPS = 1e-6
LANES = 128
BF16_SUBLANES = 16
VMEM_LIMIT_BYTES = 56 * 1024 * 1024

GLA_HEADS = 8
GLA_HEAD_K = 128
GLA_HEAD_V = 256
GLA_GATE_RANK = 16
GLA_GATE_NORM = 16.0
GLA_CHUNK = 64
MOBA_HEAD_DIM = 128
MOBA_BLOCK = 256
MOBA_TOPK = 3
MOBA_KEY_GROUP = 4
MOBA_HEADS_PER_STEP = 2
ROPE_THETA = 500000.0
ROPE_DIM = MOBA_HEAD_DIM // 4
ROPE_HALF = ROPE_DIM // 2
CROSS_HEADS = 4
MASK_VALUE = -1e30
LOG2E = 1.4426950408889634


def _params(*sem):
    return pltpu.CompilerParams(dimension_semantics=sem, vmem_limit_bytes=VMEM_LIMIT_BYTES)


def _rmsnorm_body(x_ref, g_ref, o_ref):
    x = x_ref[...].astype(F32)
    ms = jnp.mean(x * x, axis=-1, keepdims=True)
    y = x * lax.rsqrt(ms + EPS)
    o_ref[...] = (y * g_ref[...]).astype(o_ref.dtype)


def _rmsnorm(x, g, out_dtype, rows=256):
    m, d = x.shape
    rows = min(rows, m)
    return pl.pallas_call(
        _rmsnorm_body,
        grid=(m // rows,),
        in_specs=[pl.BlockSpec((rows, d), lambda i: (i, 0)),
                  pl.BlockSpec((1, d), lambda i: (0, 0))],
        out_specs=pl.BlockSpec((rows, d), lambda i: (i, 0)),
        out_shape=jax.ShapeDtypeStruct((m, d), out_dtype),
        compiler_params=_params("parallel"),
        name="rmsnorm",
    )(x, g.reshape(1, d).astype(F32))


def _mm_body(*refs, nk, act, has_res):
    if has_res:
        a_ref, w_ref, r_ref, o_ref = refs[:4]
        scratch = refs[4:]
    else:
        a_ref, w_ref, o_ref = refs[:3]
        r_ref = None
        scratch = refs[3:]

    def finish(acc):
        if act == "relu2":
            acc = jnp.square(jnp.maximum(acc, 0.0))
        if has_res:
            acc = acc + r_ref[...]
        o_ref[...] = acc.astype(o_ref.dtype)

    if nk == 1:
        finish(jnp.dot(a_ref[...], w_ref[...], preferred_element_type=F32))
        return

    acc_ref, = scratch
    k = pl.program_id(2)

    @pl.when(k == 0)
    def _():
        acc_ref[...] = jnp.dot(a_ref[...], w_ref[...], preferred_element_type=F32)

    @pl.when(k > 0)
    def _():
        acc_ref[...] += jnp.dot(a_ref[...], w_ref[...], preferred_element_type=F32)

    @pl.when(k == nk - 1)
    def _():
        finish(acc_ref[...])


def _matmul(a, w, *, out_dtype, res=None, act=None, col0=0, n=None, tm=1024, tn=1024, tk=4096,
            name="matmul"):
    m, kdim = a.shape
    n = w.shape[1] if n is None else n
    tm, tn, tk = min(tm, m), min(tn, n), min(tk, kdim)
    nk = kdim // tk
    assert col0 % tn == 0 and n % tn == 0 and m % tm == 0 and kdim % tk == 0
    j0 = col0 // tn
    in_specs = [pl.BlockSpec((tm, tk), lambda i, j, k: (i, k)),
                pl.BlockSpec((tk, tn), lambda i, j, k: (k, j0 + j))]
    args = [a, w]
    if res is not None:
        in_specs.append(pl.BlockSpec((tm, tn), lambda i, j, k: (i, j)))
        args.append(res)
    scratch = [pltpu.VMEM((tm, tn), F32)] if nk > 1 else []
    return pl.pallas_call(
        functools.partial(_mm_body, nk=nk, act=act, has_res=res is not None),
        grid=(m // tm, n // tn, nk),
        in_specs=in_specs,
        out_specs=pl.BlockSpec((tm, tn), lambda i, j, k: (i, j)),
        out_shape=jax.ShapeDtypeStruct((m, n), out_dtype),
        scratch_shapes=scratch,
        compiler_params=_params("parallel", "parallel", "arbitrary"),
        name=name,
    )(*args)


def _mm_rot_body(a_ref, w_ref, cos_ref, sin_ref, o_ref, *, scale):
    acc = jnp.dot(a_ref[...], w_ref[...], preferred_element_type=F32)
    tm, tn = acc.shape
    cos = cos_ref[...]
    sin = sin_ref[...]
    lane = lax.broadcasted_iota(jnp.int32, (tm, LANES), 1)
    for c in range(tn // LANES):
        xs = acc[:, c * LANES:(c + 1) * LANES]
        partner = jnp.where(lane < ROPE_HALF,
                            pltpu.roll(xs, LANES - ROPE_HALF, 1),
                            pltpu.roll(xs, ROPE_HALF, 1))
        y = xs * cos + partner * sin
        if scale != 1.0:
            y = y * scale
        o_ref[:, c * LANES:(c + 1) * LANES] = y.astype(o_ref.dtype)


def _matmul_rotary(a, w, cos_t, sin_t, *, scale, col0=0, n=None, tm=1024, tn=1024, name="proj_rot"):
    m, kdim = a.shape
    n = w.shape[1] if n is None else n
    seq = cos_t.shape[0]
    tm, tn = min(tm, m, seq), min(tn, n)
    tiles_per_seq = seq // tm
    assert col0 % tn == 0 and n % tn == 0 and m % tm == 0 and seq % tm == 0
    j0 = col0 // tn
    return pl.pallas_call(
        functools.partial(_mm_rot_body, scale=scale),
        grid=(m // tm, n // tn),
        in_specs=[pl.BlockSpec((tm, kdim), lambda i, j: (i, 0)),
                  pl.BlockSpec((kdim, tn), lambda i, j: (0, j0 + j)),
                  pl.BlockSpec((tm, LANES), lambda i, j: (i % tiles_per_seq, 0)),
                  pl.BlockSpec((tm, LANES), lambda i, j: (i % tiles_per_seq, 0))],
        out_specs=pl.BlockSpec((tm, tn), lambda i, j: (i, j)),
        out_shape=jax.ShapeDtypeStruct((m, n), BF16),
        compiler_params=_params("parallel", "parallel"),
        name=name,
    )(a, w, cos_t, sin_t)


def _rotary_tables(seq):
    inv_freq = ROPE_THETA ** (-jnp.arange(0, ROPE_DIM, 2, dtype=F32) / ROPE_DIM)
    ang = jnp.arange(seq).astype(F32)[:, None] * inv_freq[None, :]
    cos, sin = jnp.cos(ang), jnp.sin(ang)
    pad = MOBA_HEAD_DIM - ROPE_DIM
    cos_t = jnp.concatenate([cos, cos, jnp.ones((seq, pad), F32)], axis=-1)
    sin_t = jnp.concatenate([-sin, sin, jnp.zeros((seq, pad), F32)], axis=-1)
    return cos_t, sin_t


def _gate_body(gl_ref, w_ref, b_ref, o_ref):
    z = jnp.dot(gl_ref[...], w_ref[...], preferred_element_type=F32, precision=HIGHEST)
    z = z + b_ref[...]
    log_sig = jnp.minimum(z, 0.0) - jnp.log1p(jnp.exp(-jnp.abs(z)))
    o_ref[...] = log_sig * (1.0 / GLA_GATE_NORM)


def _gate(g_low, w_pad, b, rows=1024):
    m, r = g_low.shape
    n = w_pad.shape[1]
    rows = min(rows, m)
    return pl.pallas_call(
        _gate_body,
        grid=(m // rows,),
        in_specs=[pl.BlockSpec((rows, r), lambda i: (i, 0)),
                  pl.BlockSpec((r, n), lambda i: (0, 0)),
                  pl.BlockSpec((1, n), lambda i: (0, 0))],
        out_specs=pl.BlockSpec((rows, n), lambda i: (i, 0)),
        out_shape=jax.ShapeDtypeStruct((m, n), F32),
        compiler_params=_params("parallel"),
        name="gla_gate",
    )(g_low, w_pad, b.reshape(1, n).astype(F32))


def _gla_body(q_ref, k_ref, v_ref, g_ref, go_ref, gn_ref, o_ref, st_ref, *, chunks):
    c_len = GLA_CHUNK

    @pl.when(pl.program_id(2) == 0)
    def _():
        st_ref[...] = jnp.zeros_like(st_ref)

    row = lax.broadcasted_iota(jnp.int32, (c_len, c_len), 0)
    col = lax.broadcasted_iota(jnp.int32, (c_len, c_len), 1)
    causal = col <= row
    tri = causal.astype(F32)
    q_scale = GLA_HEAD_K ** -0.5
    gn = gn_ref[...]

    nt = (((1,), (1,)), ((), ()))
    tn = (((0,), (0,)), ((), ()))

    q_decs, o_intras, d_states, decays = [], [], [], []
    for c in range(chunks):
        sl = pl.ds(c * c_len, c_len)
        g = g_ref[sl, :]
        b = jnp.dot(tri, g, preferred_element_type=F32, precision=HIGHEST)
        b_last = b[c_len - 1:c_len, :]
        q = q_ref[sl, :].astype(F32) * q_scale
        k = k_ref[sl, :].astype(F32)
        v = v_ref[sl, :]
        q_dec = (q * jnp.exp(b)).astype(BF16)
        k_inv = (k * jnp.exp(-b)).astype(BF16)
        k_end = (k * jnp.exp(b_last - b)).astype(BF16)
        a = lax.dot_general(q_dec, k_inv, nt, preferred_element_type=F32)
        a = jnp.where(causal, a, 0.0).astype(BF16)
        q_decs.append(q_dec)
        o_intras.append(jnp.dot(a, v, preferred_element_type=F32))
        d_states.append(lax.dot_general(v, k_end, tn, preferred_element_type=F32))
        decays.append(jnp.exp(b_last))

    st = st_ref[...]
    states = []
    for c in range(chunks):
        states.append(st.astype(BF16))
        st = st * decays[c] + d_states[c]
    st_ref[...] = st

    for c in range(chunks):
        sl = pl.ds(c * c_len, c_len)
        o = o_intras[c] + lax.dot_general(q_decs[c], states[c], nt, preferred_element_type=F32)
        ms = jnp.mean(o * o, axis=-1, keepdims=True)
        y = o * lax.rsqrt(ms + EPS) * gn
        go = go_ref[sl, :].astype(F32)
        y = y * (go * jax.nn.sigmoid(go))
        o_ref[sl, :] = y.astype(o_ref.dtype)


def _gla(qkv, g, g_out, gn, *, batch, seq, heads, rows=1024):
    m = batch * seq
    rows = min(rows, seq)
    steps = seq // rows
    hk, hv = GLA_HEAD_K, GLA_HEAD_V
    k_off = heads
    v_off = heads * 2 * hk // hv
    row_map = lambda b, h, t: b * steps + t
    return pl.pallas_call(
        functools.partial(_gla_body, chunks=rows // GLA_CHUNK),
        grid=(batch, heads, steps),
        in_specs=[pl.BlockSpec((rows, hk), lambda b, h, t: (row_map(b, h, t), h)),
                  pl.BlockSpec((rows, hk), lambda b, h, t: (row_map(b, h, t), k_off + h)),
                  pl.BlockSpec((rows, hv), lambda b, h, t: (row_map(b, h, t), v_off + h)),
                  pl.BlockSpec((rows, hk), lambda b, h, t: (row_map(b, h, t), h)),
                  pl.BlockSpec((rows, hv), lambda b, h, t: (row_map(b, h, t), h)),
                  pl.BlockSpec((1, hv), lambda b, h, t: (0, 0))],
        out_specs=pl.BlockSpec((rows, hv), lambda b, h, t: (row_map(b, h, t), h)),
        out_shape=jax.ShapeDtypeStruct((m, heads * hv), BF16),
        scratch_shapes=[pltpu.VMEM((hv, hk), F32)],
        compiler_params=_params("parallel", "parallel", "arbitrary"),
        name="gla",
    )(qkv, qkv, qkv, g, g_out, gn.reshape(1, hv).astype(F32))


def _moba_body(q_ref, k_ref, v_ref, o_ref, kaug_ref, kmh_ref, kml_ref, *, n_blocks, group, hp):
    blk = MOBA_BLOCK
    dh = MOBA_HEAD_DIM
    seq = n_blocks * blk
    km_rows = kmh_ref.shape[1]
    qb = pl.program_id(2)
    nt = (((1,), (1,)), ((), ()))

    @pl.when(qb == 0)
    def _():
        r = lax.broadcasted_iota(jnp.int32, (seq, dh), 0)
        l = lax.broadcasted_iota(jnp.int32, (seq, dh), 1)
        one_hot = ((l * blk <= r) & (r < (l + 1) * blk)).astype(BF16)
        j = lax.broadcasted_iota(jnp.int32, (km_rows, seq), 0)
        s = lax.broadcasted_iota(jnp.int32, (km_rows, seq), 1)
        avg = jnp.where((j * blk <= s) & (s < (j + 1) * blk), 1.0 / blk, 0.0).astype(BF16)
        for h in range(hp):
            k = k_ref[:, h * dh:(h + 1) * dh]
            kaug_ref[h, :, :dh] = k
            kaug_ref[h, :, dh:] = one_hot
            km = jnp.dot(avg, k, preferred_element_type=F32)
            hi = km.astype(BF16)
            kmh_ref[h] = hi
            kml_ref[h] = (km - hi.astype(F32)).astype(BF16)

    row = lax.broadcasted_iota(jnp.int32, (km_rows, blk), 0)
    past = row < qb
    qi = lax.broadcasted_iota(jnp.int32, (blk, blk), 0)
    ki = lax.broadcasted_iota(jnp.int32, (blk, blk), 1)
    causal = ki <= qi
    start = pl.multiple_of(qb * blk, blk)

    def prepare(h):
        q = q_ref[:, h * dh:(h + 1) * dh]
        gate = (lax.dot_general(kmh_ref[h], q, nt, preferred_element_type=F32)
                + lax.dot_general(kml_ref[h], q, nt, preferred_element_type=F32))
        gm = jnp.where(past, gate, -jnp.inf)
        rank = jnp.zeros((km_rows, blk), jnp.int32)
        for j in range(n_blocks - 1):
            gj = gm[j:j + 1, :]
            beats = (gj > gm) | ((gj == gm) & (row > j))
            rank = rank + beats.astype(jnp.int32)
        sel = past & (rank < MOBA_TOPK)
        bias_t = jnp.where(sel, 0.0, MASK_VALUE)
        bias_t = jnp.concatenate([bias_t, jnp.zeros((dh - km_rows, blk), F32)], axis=0)
        q_aug = jnp.concatenate([q, bias_t.T.astype(BF16)], axis=1)
        k_own = k_ref[pl.ds(start, blk), h * dh:(h + 1) * dh]
        s_own = lax.dot_general(q, k_own, nt, preferred_element_type=F32)
        s_own = jnp.where(causal, s_own, MASK_VALUE)
        m_own = jnp.max(s_own, axis=-1, keepdims=True)
        return q_aug, s_own, m_own

    prepared = [prepare(h) for h in range(hp)]

    def finish(nb):
        for h in range(hp):
            q_aug, s_own, m_own = prepared[h]
            cols = slice(h * dh, (h + 1) * dh)
            v_own = v_ref[pl.ds(start, blk), cols]
            if nb == 0:
                p_own = jnp.exp2(s_own - m_own)
                l = jnp.sum(p_own, axis=-1, keepdims=True)
                acc = jnp.dot(p_own.astype(BF16), v_own, preferred_element_type=F32)
            else:
                s = lax.dot_general(q_aug, kaug_ref[h, :nb * blk, :], nt, preferred_element_type=F32)
                m = jnp.maximum(m_own, jnp.max(s, axis=-1, keepdims=True))
                p_own = jnp.exp2(s_own - m)
                p = jnp.exp2(s - m)
                l = jnp.sum(p_own, axis=-1, keepdims=True) + jnp.sum(p, axis=-1, keepdims=True)
                acc = (jnp.dot(p_own.astype(BF16), v_own, preferred_element_type=F32)
                       + jnp.dot(p.astype(BF16), v_ref[:nb * blk, cols], preferred_element_type=F32))
            o_ref[:, cols] = (acc / l).astype(o_ref.dtype)

    n_cls = -(-(n_blocks - 1) // group)
    cls = (qb + (group - 1)) // group
    for c in range(n_cls + 1):
        pl.when(cls == c)(functools.partial(finish, min(c * group, n_blocks)))


def _moba(q, k, v, *, batch, seq, heads, hp=MOBA_HEADS_PER_STEP):
    m = batch * seq
    blk, dh = MOBA_BLOCK, MOBA_HEAD_DIM
    n_blocks = seq // blk
    km_rows = -(-n_blocks // BF16_SUBLANES) * BF16_SUBLANES
    return pl.pallas_call(
        functools.partial(_moba_body, n_blocks=n_blocks, group=MOBA_KEY_GROUP, hp=hp),
        grid=(batch, heads // hp, n_blocks),
        in_specs=[pl.BlockSpec((blk, hp * dh), lambda b, h, t: (b * n_blocks + t, h)),
                  pl.BlockSpec((seq, hp * dh), lambda b, h, t: (b, h)),
                  pl.BlockSpec((seq, hp * dh), lambda b, h, t: (b, h))],
        out_specs=pl.BlockSpec((blk, hp * dh), lambda b, h, t: (b * n_blocks + t, h)),
        out_shape=jax.ShapeDtypeStruct((m, heads * dh), BF16),
        scratch_shapes=[pltpu.VMEM((hp, seq, 2 * dh), BF16),
                        pltpu.VMEM((hp, km_rows, dh), BF16),
                        pltpu.VMEM((hp, km_rows, dh), BF16)],
        compiler_params=_params("parallel", "parallel", "arbitrary"),
        name="moba",
    )(q, k, v)


def _mix_out_body(a1_ref, a2_ref, w_ref, r_ref, o_ref):
    half = a1_ref.shape[1]
    acc = jnp.dot(a1_ref[...], w_ref[:half, :], preferred_element_type=F32)
    acc = acc + jnp.dot(a2_ref[...], w_ref[half:, :], preferred_element_type=F32)
    o_ref[...] = acc + r_ref[...]


def _mix_out(a1, a2, w, res, tm=1024, tn=1024):
    m, half = a1.shape
    n = w.shape[1]
    tm, tn = min(tm, m), min(tn, n)
    return pl.pallas_call(
        _mix_out_body,
        grid=(m // tm, n // tn),
        in_specs=[pl.BlockSpec((tm, half), lambda i, j: (i, 0)),
                  pl.BlockSpec((tm, half), lambda i, j: (i, 0)),
                  pl.BlockSpec((2 * half, tn), lambda i, j: (0, j)),
                  pl.BlockSpec((tm, tn), lambda i, j: (i, j))],
        out_specs=pl.BlockSpec((tm, tn), lambda i, j: (i, j)),
        out_shape=jax.ShapeDtypeStruct((m, n), F32),
        compiler_params=_params("parallel", "parallel"),
        name="mix_out",
    )(a1, a2, w, res)


def _cross_body(q_ref, k_ref, v_ref, o_ref, *, heads):
    d = q_ref.shape[1]
    dh = d // heads
    scale = dh ** -0.5
    for h in range(heads):
        sl = slice(h * dh, (h + 1) * dh)
        s = lax.dot_general(q_ref[:, sl], k_ref[:, sl], (((1,), (1,)), ((), ())),
                            preferred_element_type=F32) * scale
        s = s - jnp.max(s, axis=-1, keepdims=True)
        e = jnp.exp(s)
        p = e / jnp.sum(e, axis=-1, keepdims=True)
        o_ref[:, sl] = jnp.dot(p.astype(BF16), v_ref[:, sl],
                               preferred_element_type=F32).astype(o_ref.dtype)


def _cross(q, k, v, *, batch, seq, n_mem, heads, rows=512):
    m, d = q.shape
    rows = min(rows, seq)
    steps = seq // rows
    return pl.pallas_call(
        functools.partial(_cross_body, heads=heads),
        grid=(batch, steps),
        in_specs=[pl.BlockSpec((rows, d), lambda b, t: (b * steps + t, 0)),
                  pl.BlockSpec((n_mem, d), lambda b, t: (b, 0)),
                  pl.BlockSpec((n_mem, d), lambda b, t: (b, 0))],
        out_specs=pl.BlockSpec((rows, d), lambda b, t: (b * steps + t, 0)),
        out_shape=jax.ShapeDtypeStruct((m, d), BF16),
        compiler_params=_params("parallel", "parallel"),
        name="cross_attn",
    )(q, k, v)


def _layer(h, mem2, batch, seq, n_mem, norm_mix_g, w_in, w_gate_up, b_gate, gla_norm_g, w_out,
           norm_cross_g, norm_mem_g, w_cq, w_ck, w_cv, w_co, norm_mlp_g, w_up, w_down):
    d = h.shape[1]
    gla_dk = GLA_HEADS * GLA_HEAD_K
    gla_dv = GLA_HEADS * GLA_HEAD_V
    moba_w = d - gla_dv
    moba_heads = moba_w // MOBA_HEAD_DIM
    o_glow = 2 * gla_dk + gla_dv
    o_gout = o_glow + GLA_GATE_RANK
    w_all = jnp.concatenate(
        [w_in[:, :o_glow], w_in[:, o_gout:],
         jnp.pad(w_in[:, o_glow:o_gout], ((0, 0), (0, LANES - GLA_GATE_RANK)))], axis=1).astype(BF16)
    c_gout = o_glow
    c_mq = c_gout + gla_dv
    c_mk = c_mq + moba_w
    c_mv = c_mk + moba_w
    c_glow = c_mv + moba_w
    w_gu = jnp.pad(w_gate_up, ((0, LANES - GLA_GATE_RANK), (0, 0)))

    xn = _rmsnorm(h, norm_mix_g, BF16)
    gla_qkv = _matmul(xn, w_all, col0=0, n=o_glow, out_dtype=BF16, name="proj_gla_qkv")
    g_low = _matmul(xn, w_all, col0=c_glow, n=LANES, out_dtype=F32, name="proj_gla_glow")
    g_out = _matmul(xn, w_all, col0=c_gout, n=gla_dv, out_dtype=BF16, name="proj_gla_gout")
    g = _gate(g_low, w_gu, b_gate)
    o_gla = _gla(gla_qkv, g, g_out, gla_norm_g, batch=batch, seq=seq, heads=GLA_HEADS)

    cos_t, sin_t = _rotary_tables(seq)
    mq = _matmul_rotary(xn, w_all, cos_t, sin_t, col0=c_mq, n=moba_w,
                        scale=MOBA_HEAD_DIM ** -0.5 * LOG2E, name="proj_moba_q")
    mk = _matmul_rotary(xn, w_all, cos_t, sin_t, col0=c_mk, n=moba_w, scale=1.0, name="proj_moba_k")
    mv = _matmul(xn, w_all, col0=c_mv, n=moba_w, out_dtype=BF16, name="proj_moba_v")
    o_moba = _moba(mq, mk, mv, batch=batch, seq=seq, heads=moba_heads)

    h = _mix_out(o_gla, o_moba, w_out.astype(BF16), h)

    hn = _rmsnorm(h, norm_cross_g, BF16)
    mem_n = _rmsnorm(mem2, norm_mem_g, BF16)
    cq = _matmul(hn, w_cq.astype(BF16), out_dtype=BF16, name="proj_cross_q")
    ck = _matmul(mem_n, w_ck.astype(BF16), out_dtype=BF16, name="proj_cross_k")
    cv = _matmul(mem_n, w_cv.astype(BF16), out_dtype=BF16, name="proj_cross_v")
    o_cross = _cross(cq, ck, cv, batch=batch, seq=seq, n_mem=n_mem, heads=CROSS_HEADS)
    h = _matmul(o_cross, w_co.astype(BF16), out_dtype=F32, res=h, name="proj_cross_out")

    hn = _rmsnorm(h, norm_mlp_g, BF16)
    u = _matmul(hn, w_up.astype(BF16), out_dtype=BF16, act="relu2", name="mlp_up")
    h = _matmul(u, w_down.astype(BF16), out_dtype=F32, res=h, tk=2048, name="mlp_down")
    return h


def kernel(x, mem, norm_mix_g, w_in, w_gate_up, b_gate, gla_norm_g, w_out, norm_cross_g, norm_mem_g,
           w_cq, w_ck, w_cv, w_co, norm_mlp_g, w_up, w_down, norm_final_g):
    batch, seq, d = x.shape
    n_mem = mem.shape[1]
    h = x.reshape(batch * seq, d)
    mem2 = mem.reshape(batch * n_mem, d)
    for l in range(norm_mix_g.shape[0]):
        h = _layer(h, mem2, batch, seq, n_mem, norm_mix_g[l], w_in[l], w_gate_up[l], b_gate[l],
                   gla_norm_g[l], w_out[l], norm_cross_g[l], norm_mem_g[l], w_cq[l], w_ck[l],
                   w_cv[l], w_co[l], norm_mlp_g[l], w_up[l], w_down[l])
    out = _rmsnorm(h, norm_final_g, x.dtype)
    return out.reshape(batch, seq, d)
```

```python
import functools

import jax
import jax.numpy as jnp
from jax import lax
from jax.experimental import pallas as pl
from jax.experimental.pallas import tpu as pltpu

F32 = jnp.float32
BF16 = jnp.bfloat16
HIGHEST = lax.Precision.HIGHEST

EPS = 1e-6
LANES = 128
BF16_SUBLANES = 16
VMEM_LIMIT_BYTES = 56 * 1024 * 1024

GLA_HEADS = 8
GLA_HEAD_K = 128
GLA_HEAD_V = 256
GLA_GATE_RANK = 16
GLA_GATE_NORM = 16.0
GLA_CHUNK = 64
MOBA_HEAD_DIM = 128
MOBA_BLOCK = 256
MOBA_TOPK = 3
MOBA_KEY_GROUP = 4
MOBA_HEADS_PER_STEP = 4
ROPE_THETA = 500000.0
ROPE_DIM = MOBA_HEAD_DIM // 4
ROPE_HALF = ROPE_DIM // 2
CROSS_HEADS = 4
MASK_VALUE = -1e30
LOG2E = 1.4426950408889634


def _params(*sem):
    return pltpu.CompilerParams(dimension_semantics=sem, vmem_limit_bytes=VMEM_LIMIT_BYTES)


def _rmsnorm_body(x_ref, g_ref, o_ref):
    x = x_ref[...].astype(F32)
    ms = jnp.mean(x * x, axis=-1, keepdims=True)
    y = x * lax.rsqrt(ms + EPS)
    o_ref[...] = (y * g_ref[...]).astype(o_ref.dtype)


def _rmsnorm(x, g, out_dtype, rows=256):
    m, d = x.shape
    rows = min(rows, m)
    return pl.pallas_call(
        _rmsnorm_body,
        grid=(m // rows,),
        in_specs=[pl.BlockSpec((rows, d), lambda i: (i, 0)),
                  pl.BlockSpec((1, d), lambda i: (0, 0))],
        out_specs=pl.BlockSpec((rows, d), lambda i: (i, 0)),
        out_shape=jax.ShapeDtypeStruct((m, d), out_dtype),
        compiler_params=_params("parallel"),
        name="rmsnorm",
    )(x, g.reshape(1, d).astype(F32))


def _mm_body(*refs, nk, act, has_res):
    if has_res:
        a_ref, w_ref, r_ref, o_ref = refs[:4]
        scratch = refs[4:]
    else:
        a_ref, w_ref, o_ref = refs[:3]
        r_ref = None
        scratch = refs[3:]

    def finish(acc):
        if act == "relu2":
            acc = jnp.square(jnp.maximum(acc, 0.0))
        if has_res:
            acc = acc + r_ref[...]
        o_ref[...] = acc.astype(o_ref.dtype)

    if nk == 1:
        finish(jnp.dot(a_ref[...], w_ref[...], preferred_element_type=F32))
        return

    acc_ref, = scratch
    k = pl.program_id(2)

    @pl.when(k == 0)
    def _():
        acc_ref[...] = jnp.dot(a_ref[...], w_ref[...], preferred_element_type=F32)

    @pl.when(k > 0)
    def _():
        acc_ref[...] += jnp.dot(a_ref[...], w_ref[...], preferred_element_type=F32)

    @pl.when(k == nk - 1)
    def _():
        finish(acc_ref[...])


def _matmul(a, w, *, out_dtype, res=None, act=None, col0=0, n=None, tm=1024, tn=1024, tk=4096,
            name="matmul"):
    m, kdim = a.shape
    n = w.shape[1] if n is None else n
    tm, tn, tk = min(tm, m), min(tn, n), min(tk, kdim)
    nk = kdim // tk
    assert col0 % tn == 0 and n % tn == 0 and m % tm == 0 and kdim % tk == 0
    j0 = col0 // tn
    in_specs = [pl.BlockSpec((tm, tk), lambda i, j, k: (i, k)),
                pl.BlockSpec((tk, tn), lambda i, j, k: (k, j0 + j))]
    args = [a, w]
    if res is not None:
        in_specs.append(pl.BlockSpec((tm, tn), lambda i, j, k: (i, j)))
        args.append(res)
    scratch = [pltpu.VMEM((tm, tn), F32)] if nk > 1 else []
    return pl.pallas_call(
        functools.partial(_mm_body, nk=nk, act=act, has_res=res is not None),
        grid=(m // tm, n // tn, nk),
        in_specs=in_specs,
        out_specs=pl.BlockSpec((tm, tn), lambda i, j, k: (i, j)),
        out_shape=jax.ShapeDtypeStruct((m, n), out_dtype),
        scratch_shapes=scratch,
        compiler_params=_params("parallel", "parallel", "arbitrary"),
        name=name,
    )(*args)


def _mm_rot_body(a_ref, w_ref, cos_ref, sin_ref, o_ref, *, scale):
    acc = jnp.dot(a_ref[...], w_ref[...], preferred_element_type=F32)
    tm, tn = acc.shape
    cos = cos_ref[...]
    sin = sin_ref[...]
    lane = lax.broadcasted_iota(jnp.int32, (tm, LANES), 1)
    for c in range(tn // LANES):
        xs = acc[:, c * LANES:(c + 1) * LANES]
        partner = jnp.where(lane < ROPE_HALF,
                            pltpu.roll(xs, LANES - ROPE_HALF, 1),
                            pltpu.roll(xs, ROPE_HALF, 1))
        y = xs * cos + partner * sin
        if scale != 1.0:
            y = y * scale
        o_ref[:, c * LANES:(c + 1) * LANES] = y.astype(o_ref.dtype)


def _matmul_rotary(a, w, cos_t, sin_t, *, scale, col0=0, n=None, tm=1024, tn=1024, name="proj_rot"):
    m, kdim = a.shape
    n = w.shape[1] if n is None else n
    seq = cos_t.shape[0]
    tm, tn = min(tm, m, seq), min(tn, n)
    tiles_per_seq = seq // tm
    assert col0 % tn == 0 and n % tn == 0 and m % tm == 0 and seq % tm == 0
    j0 = col0 // tn
    return pl.pallas_call(
        functools.partial(_mm_rot_body, scale=scale),
        grid=(m // tm, n // tn),
        in_specs=[pl.BlockSpec((tm, kdim), lambda i, j: (i, 0)),
                  pl.BlockSpec((kdim, tn), lambda i, j: (0, j0 + j)),
                  pl.BlockSpec((tm, LANES), lambda i, j: (i % tiles_per_seq, 0)),
                  pl.BlockSpec((tm, LANES), lambda i, j: (i % tiles_per_seq, 0))],
        out_specs=pl.BlockSpec((tm, tn), lambda i, j: (i, j)),
        out_shape=jax.ShapeDtypeStruct((m, n), BF16),
        compiler_params=_params("parallel", "parallel"),
        name=name,
    )(a, w, cos_t, sin_t)


def _rotary_tables(seq):
    inv_freq = ROPE_THETA ** (-jnp.arange(0, ROPE_DIM, 2, dtype=F32) / ROPE_DIM)
    ang = jnp.arange(seq).astype(F32)[:, None] * inv_freq[None, :]
    cos, sin = jnp.cos(ang), jnp.sin(ang)
    pad = MOBA_HEAD_DIM - ROPE_DIM
    cos_t = jnp.concatenate([cos, cos, jnp.ones((seq, pad), F32)], axis=-1)
    sin_t = jnp.concatenate([-sin, sin, jnp.zeros((seq, pad), F32)], axis=-1)
    return cos_t, sin_t


def _gate_body(gl_ref, w_ref, b_ref, o_ref):
    z = jnp.dot(gl_ref[...], w_ref[...], preferred_element_type=F32, precision=HIGHEST)
    z = z + b_ref[...]
    log_sig = jnp.minimum(z, 0.0) - jnp.log1p(jnp.exp(-jnp.abs(z)))
    o_ref[...] = log_sig * (1.0 / GLA_GATE_NORM)


def _gate(g_low, w_pad, b, rows=1024):
    m, r = g_low.shape
    n = w_pad.shape[1]
    rows = min(rows, m)
    return pl.pallas_call(
        _gate_body,
        grid=(m // rows,),
        in_specs=[pl.BlockSpec((rows, r), lambda i: (i, 0)),
                  pl.BlockSpec((r, n), lambda i: (0, 0)),
                  pl.BlockSpec((1, n), lambda i: (0, 0))],
        out_specs=pl.BlockSpec((rows, n), lambda i: (i, 0)),
        out_shape=jax.ShapeDtypeStruct((m, n), F32),
        compiler_params=_params("parallel"),
        name="gla_gate",
    )(g_low, w_pad, b.reshape(1, n).astype(F32))


def _gla_body(q_ref, k_ref, v_ref, g_ref, go_ref, gn_ref, o_ref, st_ref, *, chunks):
    c_len = GLA_CHUNK

    @pl.when(pl.program_id(2) == 0)
    def _():
        st_ref[...] = jnp.zeros_like(st_ref)

    row = lax.broadcasted_iota(jnp.int32, (c_len, c_len), 0)
    col = lax.broadcasted_iota(jnp.int32, (c_len, c_len), 1)
    causal = col <= row
    tri = causal.astype(F32)
    q_scale = GLA_HEAD_K ** -0.5
    gn = gn_ref[...]

    nt = (((1,), (1,)), ((), ()))
    tn = (((0,), (0,)), ((), ()))

    dk = g_ref.shape[1]
    g_all = jnp.concatenate([g_ref[pl.ds(c * c_len, c_len), :] for c in range(chunks)], axis=1)
    b_all = jnp.dot(tri, g_all, preferred_element_type=F32, precision=HIGHEST)
    b_last_all = b_all[c_len - 1:c_len, :]
    e_pos = jnp.exp(b_all)
    e_neg = jnp.exp(-b_all)
    e_end = jnp.exp(b_last_all - b_all)
    decay_all = jnp.exp(b_last_all)

    q_decs, o_intras, d_states, decays = [], [], [], []
    for c in range(chunks):
        sl = pl.ds(c * c_len, c_len)
        lanes = slice(c * dk, (c + 1) * dk)
        q = q_ref[sl, :].astype(F32) * q_scale
        k = k_ref[sl, :].astype(F32)
        v = v_ref[sl, :]
        q_dec = (q * e_pos[:, lanes]).astype(BF16)
        k_inv = (k * e_neg[:, lanes]).astype(BF16)
        k_end = (k * e_end[:, lanes]).astype(BF16)
        a = lax.dot_general(q_dec, k_inv, nt, preferred_element_type=F32)
        a = jnp.where(causal, a, 0.0).astype(BF16)
        q_decs.append(q_dec)
        o_intras.append(jnp.dot(a, v, preferred_element_type=F32))
        d_states.append(lax.dot_general(v, k_end, tn, preferred_element_type=F32))
        decays.append(decay_all[:, lanes])

    st = st_ref[...]
    states = []
    for c in range(chunks):
        states.append(st.astype(BF16))
        st = st * decays[c] + d_states[c]
    st_ref[...] = st

    for c in range(chunks):
        sl = pl.ds(c * c_len, c_len)
        o = o_intras[c] + lax.dot_general(q_decs[c], states[c], nt, preferred_element_type=F32)
        ms = jnp.mean(o * o, axis=-1, keepdims=True)
        y = o * lax.rsqrt(ms + EPS) * gn
        go = go_ref[sl, :].astype(F32)
        y = y * (go * jax.nn.sigmoid(go))
        o_ref[sl, :] = y.astype(o_ref.dtype)


def _gla(qkv, g, g_out, gn, *, batch, seq, heads, rows=1024):
    m = batch * seq
    rows = min(rows, seq)
    steps = seq // rows
    hk, hv = GLA_HEAD_K, GLA_HEAD_V
    k_off = heads
    v_off = heads * 2 * hk // hv
    row_map = lambda b, h, t: b * steps + t
    return pl.pallas_call(
        functools.partial(_gla_body, chunks=rows // GLA_CHUNK),
        grid=(batch, heads, steps),
        in_specs=[pl.BlockSpec((rows, hk), lambda b, h, t: (row_map(b, h, t), h)),
                  pl.BlockSpec((rows, hk), lambda b, h, t: (row_map(b, h, t), k_off + h)),
                  pl.BlockSpec((rows, hv), lambda b, h, t: (row_map(b, h, t), v_off + h)),
                  pl.BlockSpec((rows, hk), lambda b, h, t: (row_map(b, h, t), h)),
                  pl.BlockSpec((rows, hv), lambda b, h, t: (row_map(b, h, t), h)),
                  pl.BlockSpec((1, hv), lambda b, h, t: (0, 0))],
        out_specs=pl.BlockSpec((rows, hv), lambda b, h, t: (row_map(b, h, t), h)),
        out_shape=jax.ShapeDtypeStruct((m, heads * hv), BF16),
        scratch_shapes=[pltpu.VMEM((hv, hk), F32)],
        compiler_params=_params("parallel", "parallel", "arbitrary"),
        name="gla",
    )(qkv, qkv, qkv, g, g_out, gn.reshape(1, hv).astype(F32))


def _moba_body(q_ref, qn_ref, k_ref, v_ref, o_ref, kaug_ref, vaug_ref, kmh_ref, kml_ref, bias_ref,
               *, n_blocks, group, hp):
    blk = MOBA_BLOCK
    dh = MOBA_HEAD_DIM
    seq = n_blocks * blk
    km_rows = kmh_ref.shape[1]
    qb = pl.program_id(2)
    nt = (((1,), (1,)), ((), ()))

    @pl.when(qb == 0)
    def _():
        r = lax.broadcasted_iota(jnp.int32, (seq, dh), 0)
        l = lax.broadcasted_iota(jnp.int32, (seq, dh), 1)
        one_hot = ((l * blk <= r) & (r < (l + 1) * blk)).astype(BF16)
        ones = jnp.ones((seq, dh), BF16)
        j = lax.broadcasted_iota(jnp.int32, (km_rows, seq), 0)
        s = lax.broadcasted_iota(jnp.int32, (km_rows, seq), 1)
        avg = jnp.where((j * blk <= s) & (s < (j + 1) * blk), 1.0 / blk, 0.0).astype(BF16)
        for h in range(hp):
            k = k_ref[:, h * dh:(h + 1) * dh]
            kaug_ref[h, :, :dh] = k
            kaug_ref[h, :, dh:] = one_hot
            vaug_ref[h, :, :dh] = v_ref[:, h * dh:(h + 1) * dh]
            vaug_ref[h, :, dh:] = ones
            km = jnp.dot(avg, k, preferred_element_type=F32)
            hi = km.astype(BF16)
            kmh_ref[h] = hi
            kml_ref[h] = (km - hi.astype(F32)).astype(BF16)

    row = lax.broadcasted_iota(jnp.int32, (km_rows, blk), 0)
    qi = lax.broadcasted_iota(jnp.int32, (blk, blk), 0)
    ki = lax.broadcasted_iota(jnp.int32, (blk, blk), 1)
    causal = ki <= qi
    start = pl.multiple_of(qb * blk, blk)
    slot = qb % 2

    def select_next(h):
        qn = qn_ref[:, h * dh:(h + 1) * dh]
        gate = (lax.dot_general(kmh_ref[h], qn, nt, preferred_element_type=F32)
                + lax.dot_general(kml_ref[h], qn, nt, preferred_element_type=F32))
        past = row < qb + 1
        gm = jnp.where(past, gate, -jnp.inf)
        rank = jnp.zeros((km_rows, blk), jnp.int32)
        for j in range(n_blocks - 1):
            gj = gm[j:j + 1, :]
            beats = (gj > gm) | ((gj == gm) & (row > j))
            rank = rank + beats.astype(jnp.int32)
        sel = past & (rank < MOBA_TOPK)
        bias_t = jnp.where(sel, 0.0, MASK_VALUE)
        bias_t = jnp.concatenate([bias_t, jnp.zeros((dh - km_rows, blk), F32)], axis=0)
        bias_ref[1 - slot, h] = bias_t.T.astype(BF16)

    def attend(nb):
        for h in range(hp):
            cols = slice(h * dh, (h + 1) * dh)
            q = q_ref[:, cols]
            s_own = lax.dot_general(q, kaug_ref[h, pl.ds(start, blk), :dh], nt, preferred_element_type=F32)
            s_own = jnp.where(causal, s_own, MASK_VALUE)
            m = jnp.max(s_own, axis=-1, keepdims=True)
            v_own = vaug_ref[h, pl.ds(start, blk), :]
            if nb == 0:
                p_own = jnp.exp2((s_own - m).astype(BF16))
                acc = jnp.dot(p_own, v_own, preferred_element_type=F32)
            else:
                q_aug = jnp.concatenate([q, bias_ref[slot, h]], axis=1)
                s = lax.dot_general(q_aug, kaug_ref[h, :nb * blk, :], nt, preferred_element_type=F32)
                m = jnp.maximum(m, jnp.max(s, axis=-1, keepdims=True))
                p_own = jnp.exp2((s_own - m).astype(BF16))
                p = jnp.exp2((s - m).astype(BF16))
                acc = (jnp.dot(p_own, v_own, preferred_element_type=F32)
                       + jnp.dot(p, vaug_ref[h, :nb * blk, :], preferred_element_type=F32))
            o_ref[:, cols] = (acc[:, :dh] / acc[:, dh:dh + 1]).astype(o_ref.dtype)
        for h in range(hp):
            select_next(h)

    n_cls = -(-(n_blocks - 1) // group)
    cls = (qb + (group - 1)) // group
    for c in range(n_cls + 1):
        pl.when(cls == c)(functools.partial(attend, min(c * group, n_blocks)))


def _moba(q, k, v, *, batch, seq, heads, hp=MOBA_HEADS_PER_STEP):
    m = batch * seq
    blk, dh = MOBA_BLOCK, MOBA_HEAD_DIM
    n_blocks = seq // blk
    km_rows = -(-n_blocks // BF16_SUBLANES) * BF16_SUBLANES
    return pl.pallas_call(
        functools.partial(_moba_body, n_blocks=n_blocks, group=MOBA_KEY_GROUP, hp=hp),
        grid=(batch, heads // hp, n_blocks),
        in_specs=[pl.BlockSpec((blk, hp * dh), lambda b, h, t: (b * n_blocks + t, h)),
                  pl.BlockSpec((blk, hp * dh),
                               lambda b, h, t: (b * n_blocks + jnp.minimum(t + 1, n_blocks - 1), h)),
                  pl.BlockSpec((seq, hp * dh), lambda b, h, t: (b, h)),
                  pl.BlockSpec((seq, hp * dh), lambda b, h, t: (b, h))],
        out_specs=pl.BlockSpec((blk, hp * dh), lambda b, h, t: (b * n_blocks + t, h)),
        out_shape=jax.ShapeDtypeStruct((m, heads * dh), BF16),
        scratch_shapes=[pltpu.VMEM((hp, seq, 2 * dh), BF16),
                        pltpu.VMEM((hp, seq, 2 * dh), BF16),
                        pltpu.VMEM((hp, km_rows, dh), BF16),
                        pltpu.VMEM((hp, km_rows, dh), BF16),
                        pltpu.VMEM((2, hp, blk, dh), BF16)],
        compiler_params=_params("parallel", "parallel", "arbitrary"),
        name="moba",
    )(q, q, k, v)


def _mix_out_body(a1_ref, a2_ref, w_ref, r_ref, o_ref):
    half = a1_ref.shape[1]
    acc = jnp.dot(a1_ref[...], w_ref[:half, :], preferred_element_type=F32)
    acc = acc + jnp.dot(a2_ref[...], w_ref[half:, :], preferred_element_type=F32)
    o_ref[...] = acc + r_ref[...]


def _mix_out(a1, a2, w, res, tm=1024, tn=1024):
    m, half = a1.shape
    n = w.shape[1]
    tm, tn = min(tm, m), min(tn, n)
    return pl.pallas_call(
        _mix_out_body,
        grid=(m // tm, n // tn),
        in_specs=[pl.BlockSpec((tm, half), lambda i, j: (i, 0)),
                  pl.BlockSpec((tm, half), lambda i, j: (i, 0)),
                  pl.BlockSpec((2 * half, tn), lambda i, j: (0, j)),
                  pl.BlockSpec((tm, tn), lambda i, j: (i, j))],
        out_specs=pl.BlockSpec((tm, tn), lambda i, j: (i, j)),
        out_shape=jax.ShapeDtypeStruct((m, n), F32),
        compiler_params=_params("parallel", "parallel"),
        name="mix_out",
    )(a1, a2, w, res)


def _cross_body(q_ref, k_ref, v_ref, o_ref, *, heads):
    d = q_ref.shape[1]
    dh = d // heads
    scale = dh ** -0.5
    for h in range(heads):
        sl = slice(h * dh, (h + 1) * dh)
        s = lax.dot_general(q_ref[:, sl], k_ref[:, sl], (((1,), (1,)), ((), ())),
                            preferred_element_type=F32) * scale
        s = s - jnp.max(s, axis=-1, keepdims=True)
        e = jnp.exp(s)
        p = e / jnp.sum(e, axis=-1, keepdims=True)
        o_ref[:, sl] = jnp.dot(p.astype(BF16), v_ref[:, sl],
                               preferred_element_type=F32).astype(o_ref.dtype)


def _cross(q, k, v, *, batch, seq, n_mem, heads, rows=512):
    m, d = q.shape
    rows = min(rows, seq)
    steps = seq // rows
    return pl.pallas_call(
        functools.partial(_cross_body, heads=heads),
        grid=(batch, steps),
        in_specs=[pl.BlockSpec((rows, d), lambda b, t: (b * steps + t, 0)),
                  pl.BlockSpec((n_mem, d), lambda b, t: (b, 0)),
                  pl.BlockSpec((n_mem, d), lambda b, t: (b, 0))],
        out_specs=pl.BlockSpec((rows, d), lambda b, t: (b * steps + t, 0)),
        out_shape=jax.ShapeDtypeStruct((m, d), BF16),
        compiler_params=_params("parallel", "parallel"),
        name="cross_attn",
    )(q, k, v)


def _layer(h, mem2, batch, seq, n_mem, norm_mix_g, w_in, w_gate_up, b_gate, gla_norm_g, w_out,
           norm_cross_g, norm_mem_g, w_cq, w_ck, w_cv, w_co, norm_mlp_g, w_up, w_down):
    d = h.shape[1]
    gla_dk = GLA_HEADS * GLA_HEAD_K
    gla_dv = GLA_HEADS * GLA_HEAD_V
    moba_w = d - gla_dv
    moba_heads = moba_w // MOBA_HEAD_DIM
    o_glow = 2 * gla_dk + gla_dv
    o_gout = o_glow + GLA_GATE_RANK
    w_all = jnp.concatenate(
        [w_in[:, :o_glow], w_in[:, o_gout:],
         jnp.pad(w_in[:, o_glow:o_gout], ((0, 0), (0, LANES - GLA_GATE_RANK)))], axis=1).astype(BF16)
    c_gout = o_glow
    c_mq = c_gout + gla_dv
    c_mk = c_mq + moba_w
    c_mv = c_mk + moba_w
    c_glow = c_mv + moba_w
    w_gu = jnp.pad(w_gate_up, ((0, LANES - GLA_GATE_RANK), (0, 0)))

    xn = _rmsnorm(h, norm_mix_g, BF16)
    gla_qkv = _matmul(xn, w_all, col0=0, n=o_glow, out_dtype=BF16, name="proj_gla_qkv")
    g_low = _matmul(xn, w_all, col0=c_glow, n=LANES, out_dtype=F32, name="proj_gla_glow")
    g_out = _matmul(xn, w_all, col0=c_gout, n=gla_dv, out_dtype=BF16, name="proj_gla_gout")
    g = _gate(g_low, w_gu, b_gate)
    o_gla = _gla(gla_qkv, g, g_out, gla_norm_g, batch=batch, seq=seq, heads=GLA_HEADS)

    cos_t, sin_t = _rotary_tables(seq)
    mq = _matmul_rotary(xn, w_all, cos_t, sin_t, col0=c_mq, n=moba_w,
                        scale=MOBA_HEAD_DIM ** -0.5 * LOG2E, name="proj_moba_q")
    mk = _matmul_rotary(xn, w_all, cos_t, sin_t, col0=c_mk, n=moba_w, scale=1.0, name="proj_moba_k")
    mv = _matmul(xn, w_all, col0=c_mv, n=moba_w, out_dtype=BF16, name="proj_moba_v")
    o_moba = _moba(mq, mk, mv, batch=batch, seq=seq, heads=moba_heads)

    h = _mix_out(o_gla, o_moba, w_out.astype(BF16), h)

    hn = _rmsnorm(h, norm_cross_g, BF16)
    mem_n = _rmsnorm(mem2, norm_mem_g, BF16)
    cq = _matmul(hn, w_cq.astype(BF16), out_dtype=BF16, name="proj_cross_q")
    ck = _matmul(mem_n, w_ck.astype(BF16), out_dtype=BF16, name="proj_cross_k")
    cv = _matmul(mem_n, w_cv.astype(BF16), out_dtype=BF16, name="proj_cross_v")
    o_cross = _cross(cq, ck, cv, batch=batch, seq=seq, n_mem=n_mem, heads=CROSS_HEADS)
    h = _matmul(o_cross, w_co.astype(BF16), out_dtype=F32, res=h, name="proj_cross_out")

    hn = _rmsnorm(h, norm_mlp_g, BF16)
    u = _matmul(hn, w_up.astype(BF16), out_dtype=BF16, act="relu2", name="mlp_up")
    h = _matmul(u, w_down.astype(BF16), out_dtype=F32, res=h, tk=2048, name="mlp_down")
    return h


def kernel(x, mem, norm_mix_g, w_in, w_gate_up, b_gate, gla_norm_g, w_out, norm_cross_g, norm_mem_g,
           w_cq, w_ck, w_cv, w_co, norm_mlp_g, w_up, w_down, norm_final_g):
    batch, seq, d = x.shape
    n_mem = mem.shape[1]
    h = x.reshape(batch * seq, d)
    mem2 = mem.reshape(batch * n_mem, d)
    for l in range(norm_mix_g.shape[0]):
        h = _layer(h, mem2, batch, seq, n_mem, norm_mix_g[l], w_in[l], w_gate_up[l], b_gate[l],
                   gla_norm_g[l], w_out[l], norm_cross_g[l], norm_mem_g[l], w_cq[l], w_ck[l],
                   w_cv[l], w_co[l], norm_mlp_g[l], w_up[l], w_down[l])
    out = _rmsnorm(h, norm_final_g, x.dtype)
    return out.reshape(batch, seq, d)
```

```python
import functools

import jax
import jax.numpy as jnp
from jax import lax
from jax.experimental import pallas as pl
from jax.experimental.pallas import tpu as pltpu

F32 = jnp.float32
BF16 = jnp.bfloat16
HIGHEST = lax.Precision.HIGHEST

EPS = 1e-6
LANES = 128
BF16_SUBLANES = 16
VMEM_LIMIT_BYTES = 56 * 1024 * 1024

GLA_HEADS = 8
GLA_HEAD_K = 128
GLA_HEAD_V = 256
GLA_GATE_RANK = 16
GLA_GATE_NORM = 16.0
GLA_CHUNK = 64
MOBA_HEAD_DIM = 128
MOBA_BLOCK = 256
MOBA_TOPK = 3
MOBA_KEY_GROUP = 4
MOBA_HEADS_PER_STEP = 4
ROPE_THETA = 500000.0
ROPE_DIM = MOBA_HEAD_DIM // 4
ROPE_HALF = ROPE_DIM // 2
CROSS_HEADS = 4
MASK_VALUE = -1e30
LOG2E = 1.4426950408889634


def _params(*sem):
    return pltpu.CompilerParams(dimension_semantics=sem, vmem_limit_bytes=VMEM_LIMIT_BYTES)


def _rmsnorm_body(x_ref, g_ref, *rest):
    o_ref = rest[-1] if len(rest) == 1 else rest[1]
    x = x_ref[...].astype(F32)
    ms = jnp.mean(x * x, axis=-1, keepdims=True)
    y = x * lax.rsqrt(ms + EPS)
    o_ref[...] = (y * g_ref[...]).astype(o_ref.dtype)
    if len(rest) == 3:
        rest[2][...] = rest[0][...].astype(BF16)


def _rmsnorm(x, g, out_dtype, rows=256, side=None):
    m, d = x.shape
    rows = min(rows, m)
    steps = m // rows
    in_specs = [pl.BlockSpec((rows, d), lambda i: (i, 0)),
                pl.BlockSpec((1, d), lambda i: (0, 0))]
    out_specs = [pl.BlockSpec((rows, d), lambda i: (i, 0))]
    out_shape = [jax.ShapeDtypeStruct((m, d), out_dtype)]
    args = [x, g.reshape(1, d).astype(F32)]
    if side is not None:
        src, width = side
        assert src.shape[0] == steps and width % LANES == 0
        r = src.shape[1]
        in_specs.append(pl.BlockSpec((1, r, width), lambda i: (i, 0, 0)))
        out_specs.append(pl.BlockSpec((1, r, width), lambda i: (i, 0, 0)))
        out_shape.append(jax.ShapeDtypeStruct((steps, r, width), BF16))
        args.append(src)
    outs = pl.pallas_call(
        _rmsnorm_body,
        grid=(steps,),
        in_specs=in_specs,
        out_specs=out_specs,
        out_shape=out_shape,
        compiler_params=_params("parallel"),
        name="rmsnorm",
    )(*args)
    if side is None:
        return outs[0]
    return outs[0], outs[1].reshape(-1, width)


def _convert_side(src_ref, out_refs, windows):
    x = src_ref[0]
    if windows is None:
        out_refs[0][0] = x.astype(BF16)
        return
    for o_ref, (c0, c1) in zip(out_refs, windows):
        width = o_ref.shape[2]
        if c1 - c0 == width:
            o_ref[0] = x[:, c0:c1].astype(BF16)
        else:
            piece = x[:, c0:c0 + width]
            lane = lax.broadcasted_iota(jnp.int32, piece.shape, 1)
            o_ref[0] = jnp.where(lane < c1 - c0, piece, 0.0).astype(BF16)


def _mm_body(*refs, nk, act, has_res, rot_scale, sides):
    it = iter(refs)
    a_ref, w_ref = next(it), next(it)
    r_ref = next(it) if has_res else None
    cos_ref, sin_ref = (next(it), next(it)) if rot_scale is not None else (None, None)
    side_in = [next(it) for _ in sides]
    o_ref = next(it)
    side_out = [[next(it) for _ in range(1 if windows is None else len(windows))] for windows in sides]
    acc_ref = next(it) if nk > 1 else None

    for src_ref, out_refs, windows in zip(side_in, side_out, sides):
        _convert_side(src_ref, out_refs, windows)

    def finish(acc):
        if act == "relu2":
            acc = jnp.square(jnp.maximum(acc, 0.0))
        if has_res:
            acc = acc + r_ref[...]
        if rot_scale is None:
            o_ref[...] = acc.astype(o_ref.dtype)
            return
        tm, tn = acc.shape
        cos = cos_ref[...]
        sin = sin_ref[...]
        lane = lax.broadcasted_iota(jnp.int32, (tm, LANES), 1)
        for c in range(tn // LANES):
            xs = acc[:, c * LANES:(c + 1) * LANES]
            partner = jnp.where(lane < ROPE_HALF,
                                pltpu.roll(xs, LANES - ROPE_HALF, 1),
                                pltpu.roll(xs, ROPE_HALF, 1))
            y = xs * cos + partner * sin
            if rot_scale != 1.0:
                y = y * rot_scale
            o_ref[:, c * LANES:(c + 1) * LANES] = y.astype(o_ref.dtype)

    if nk == 1:
        finish(jnp.dot(a_ref[...], w_ref[...], preferred_element_type=F32))
        return

    k = pl.program_id(2)

    @pl.when(k == 0)
    def _():
        acc_ref[...] = jnp.dot(a_ref[...], w_ref[...], preferred_element_type=F32)

    @pl.when(k > 0)
    def _():
        acc_ref[...] += jnp.dot(a_ref[...], w_ref[...], preferred_element_type=F32)

    @pl.when(k == nk - 1)
    def _():
        finish(acc_ref[...])


def _matmul(a, w, *, out_dtype, res=None, act=None, rot=None, col0=0, n=None, sides=(),
            tm=1024, tn=1024, tk=4096, name="matmul"):
    m, kdim = a.shape
    n = w.shape[1] if n is None else n
    tm, tn, tk = min(tm, m), min(tn, n), min(tk, kdim)
    if rot is not None:
        tm = min(tm, rot[0].shape[0])
    nk = kdim // tk
    assert col0 % tn == 0 and n % tn == 0 and m % tm == 0 and kdim % tk == 0
    j0 = col0 // tn
    nj = n // tn
    steps = (m // tm) * nj * nk
    in_specs = [pl.BlockSpec((tm, tk), lambda i, j, k: (i, k)),
                pl.BlockSpec((tk, tn), lambda i, j, k: (k, j0 + j))]
    args = [a, w]
    if res is not None:
        in_specs.append(pl.BlockSpec((tm, tn), lambda i, j, k: (i, j)))
        args.append(res)
    if rot is not None:
        cos_t, sin_t, rot_scale = rot
        assert nk == 1 and cos_t.shape[0] % tm == 0
        tiles_per_seq = cos_t.shape[0] // tm
        in_specs += [pl.BlockSpec((tm, LANES), lambda i, j, k: (i % tiles_per_seq, 0))] * 2
        args += [cos_t, sin_t]
    else:
        rot_scale = None
    step = lambda i, j, k: ((i * nj + j) * nk + k, 0, 0)
    out_specs = [pl.BlockSpec((tm, tn), lambda i, j, k: (i, j))]
    out_shape = [jax.ShapeDtypeStruct((m, n), out_dtype)]
    side_desc = []
    for src, windows in sides:
        assert src.shape[0] == steps and src.dtype == F32
        rows, cols = src.shape[1:]
        in_specs.append(pl.BlockSpec((1, rows, cols), step))
        args.append(src)
        widths = [cols] if windows is None else [wd for _, _, wd in windows]
        for wd in widths:
            out_specs.append(pl.BlockSpec((1, rows, wd), step))
            out_shape.append(jax.ShapeDtypeStruct((steps, rows, wd), BF16))
        side_desc.append(None if windows is None else tuple((c0, c1) for c0, c1, _ in windows))
    scratch = [pltpu.VMEM((tm, tn), F32)] if nk > 1 else []
    outs = pl.pallas_call(
        functools.partial(_mm_body, nk=nk, act=act, has_res=res is not None, rot_scale=rot_scale,
                          sides=tuple(side_desc)),
        grid=(m // tm, nj, nk),
        in_specs=in_specs,
        out_specs=out_specs,
        out_shape=out_shape,
        scratch_shapes=scratch,
        compiler_params=_params("parallel", "parallel", "arbitrary"),
        name=name,
    )(*args)
    if not sides:
        return outs[0]
    return outs[0], [o.reshape(-1, o.shape[2]) for o in outs[1:]]


def _row_blocks(w, steps):
    rows, cols = w.shape
    assert rows % steps == 0 and (rows // steps) % BF16_SUBLANES == 0
    return w.reshape(steps, rows // steps, cols)


def _rotary_tables(seq):
    inv_freq = ROPE_THETA ** (-jnp.arange(0, ROPE_DIM, 2, dtype=F32) / ROPE_DIM)
    ang = jnp.arange(seq).astype(F32)[:, None] * inv_freq[None, :]
    cos, sin = jnp.cos(ang), jnp.sin(ang)
    pad = MOBA_HEAD_DIM - ROPE_DIM
    cos_t = jnp.concatenate([cos, cos, jnp.ones((seq, pad), F32)], axis=-1)
    sin_t = jnp.concatenate([-sin, sin, jnp.zeros((seq, pad), F32)], axis=-1)
    return cos_t, sin_t


def _gate_body(a_ref, wl_ref, wu_ref, b_ref, o_ref):
    g_low = jnp.dot(a_ref[...], wl_ref[...], preferred_element_type=F32)
    z = jnp.dot(g_low, wu_ref[...], preferred_element_type=F32, precision=HIGHEST)
    z = z + b_ref[...]
    log_sig = jnp.minimum(z, 0.0) - jnp.log1p(jnp.exp(-jnp.abs(z)))
    o_ref[...] = log_sig * (1.0 / GLA_GATE_NORM)


def _gate(xn, w_low, w_up_pad, b, rows=1024):
    m, d = xn.shape
    r = w_low.shape[1]
    n = w_up_pad.shape[1]
    rows = min(rows, m)
    return pl.pallas_call(
        _gate_body,
        grid=(m // rows,),
        in_specs=[pl.BlockSpec((rows, d), lambda i: (i, 0)),
                  pl.BlockSpec((d, r), lambda i: (0, 0)),
                  pl.BlockSpec((r, n), lambda i: (0, 0)),
                  pl.BlockSpec((1, n), lambda i: (0, 0))],
        out_specs=pl.BlockSpec((rows, n), lambda i: (i, 0)),
        out_shape=jax.ShapeDtypeStruct((m, n), F32),
        compiler_params=_params("parallel"),
        name="gla_gate",
    )(xn, w_low, w_up_pad, b.reshape(1, n).astype(F32))


def _gla_body(q_ref, k_ref, v_ref, g_ref, go_ref, gn_ref, o_ref, st_ref, *, chunks):
    c_len = GLA_CHUNK

    @pl.when(pl.program_id(2) == 0)
    def _():
        st_ref[...] = jnp.zeros_like(st_ref)

    row = lax.broadcasted_iota(jnp.int32, (c_len, c_len), 0)
    col = lax.broadcasted_iota(jnp.int32, (c_len, c_len), 1)
    causal = col <= row
    tri = causal.astype(F32)
    q_scale = GLA_HEAD_K ** -0.5
    gn = gn_ref[...]

    nt = (((1,), (1,)), ((), ()))
    tn = (((0,), (0,)), ((), ()))

    dk = g_ref.shape[1]
    g_all = jnp.concatenate([g_ref[pl.ds(c * c_len, c_len), :] for c in range(chunks)], axis=1)
    b_all = jnp.dot(tri, g_all, preferred_element_type=F32, precision=HIGHEST)
    b_last_all = b_all[c_len - 1:c_len, :]
    e_pos = jnp.exp(b_all)
    e_neg = jnp.exp(-b_all)
    e_end = jnp.exp(b_last_all - b_all)
    decay_all = jnp.exp(b_last_all)

    q_decs, o_intras, d_states, decays = [], [], [], []
    for c in range(chunks):
        sl = pl.ds(c * c_len, c_len)
        lanes = slice(c * dk, (c + 1) * dk)
        q = q_ref[sl, :].astype(F32) * q_scale
        k = k_ref[sl, :].astype(F32)
        v = v_ref[sl, :]
        q_dec = (q * e_pos[:, lanes]).astype(BF16)
        k_inv = (k * e_neg[:, lanes]).astype(BF16)
        k_end = (k * e_end[:, lanes]).astype(BF16)
        a = lax.dot_general(q_dec, k_inv, nt, preferred_element_type=F32)
        a = jnp.where(causal, a, 0.0).astype(BF16)
        q_decs.append(q_dec)
        o_intras.append(jnp.dot(a, v, preferred_element_type=F32))
        d_states.append(lax.dot_general(v, k_end, tn, preferred_element_type=F32))
        decays.append(decay_all[:, lanes])

    st = st_ref[...]
    states = []
    for c in range(chunks):
        states.append(st.astype(BF16))
        st = st * decays[c] + d_states[c]
    st_ref[...] = st

    for c in range(chunks):
        sl = pl.ds(c * c_len, c_len)
        o = o_intras[c] + lax.dot_general(q_decs[c], states[c], nt, preferred_element_type=F32)
        ms = jnp.mean(o * o, axis=-1, keepdims=True)
        y = o * lax.rsqrt(ms + EPS) * gn
        go = go_ref[sl, :].astype(F32)
        y = y * (go * jax.nn.sigmoid(go))
        o_ref[sl, :] = y.astype(o_ref.dtype)


def _gla(qkv, g, g_out, gn, *, batch, seq, heads, rows=1024):
    m = batch * seq
    rows = min(rows, seq)
    steps = seq // rows
    hk, hv = GLA_HEAD_K, GLA_HEAD_V
    k_off = heads
    v_off = heads * 2 * hk // hv
    row_map = lambda b, h, t: b * steps + t
    return pl.pallas_call(
        functools.partial(_gla_body, chunks=rows // GLA_CHUNK),
        grid=(batch, heads, steps),
        in_specs=[pl.BlockSpec((rows, hk), lambda b, h, t: (row_map(b, h, t), h)),
                  pl.BlockSpec((rows, hk), lambda b, h, t: (row_map(b, h, t), k_off + h)),
                  pl.BlockSpec((rows, hv), lambda b, h, t: (row_map(b, h, t), v_off + h)),
                  pl.BlockSpec((rows, hk), lambda b, h, t: (row_map(b, h, t), h)),
                  pl.BlockSpec((rows, hv), lambda b, h, t: (row_map(b, h, t), h)),
                  pl.BlockSpec((1, hv), lambda b, h, t: (0, 0))],
        out_specs=pl.BlockSpec((rows, hv), lambda b, h, t: (row_map(b, h, t), h)),
        out_shape=jax.ShapeDtypeStruct((m, heads * hv), BF16),
        scratch_shapes=[pltpu.VMEM((hv, hk), F32)],
        compiler_params=_params("parallel", "parallel", "arbitrary"),
        name="gla",
    )(qkv, qkv, qkv, g, g_out, gn.reshape(1, hv).astype(F32))


def _moba_body(q_ref, qn_ref, k_ref, v_ref, o_ref, kaug_ref, vaug_ref, kmh_ref, kml_ref, bias_ref,
               *, n_blocks, group, hp):
    blk = MOBA_BLOCK
    dh = MOBA_HEAD_DIM
    seq = n_blocks * blk
    km_rows = kmh_ref.shape[1]
    qb = pl.program_id(2)
    nt = (((1,), (1,)), ((), ()))

    @pl.when(qb == 0)
    def _():
        r = lax.broadcasted_iota(jnp.int32, (seq, dh), 0)
        l = lax.broadcasted_iota(jnp.int32, (seq, dh), 1)
        one_hot = ((l * blk <= r) & (r < (l + 1) * blk)).astype(BF16)
        ones = jnp.ones((seq, dh), BF16)
        j = lax.broadcasted_iota(jnp.int32, (km_rows, seq), 0)
        s = lax.broadcasted_iota(jnp.int32, (km_rows, seq), 1)
        avg = jnp.where((j * blk <= s) & (s < (j + 1) * blk), 1.0 / blk, 0.0).astype(BF16)
        for h in range(hp):
            k = k_ref[:, h * dh:(h + 1) * dh]
            kaug_ref[h, :, :dh] = k
            kaug_ref[h, :, dh:] = one_hot
            vaug_ref[h, :, :dh] = v_ref[:, h * dh:(h + 1) * dh]
            vaug_ref[h, :, dh:] = ones
            km = jnp.dot(avg, k, preferred_element_type=F32)
            hi = km.astype(BF16)
            kmh_ref[h] = hi
            kml_ref[h] = (km - hi.astype(F32)).astype(BF16)

    row = lax.broadcasted_iota(jnp.int32, (km_rows, blk), 0)
    qi = lax.broadcasted_iota(jnp.int32, (blk, blk), 0)
    ki = lax.broadcasted_iota(jnp.int32, (blk, blk), 1)
    causal = ki <= qi
    start = pl.multiple_of(qb * blk, blk)
    slot = qb % 2

    def select_next(h):
        qn = qn_ref[:, h * dh:(h + 1) * dh]
        gate = (lax.dot_general(kmh_ref[h], qn, nt, preferred_element_type=F32)
                + lax.dot_general(kml_ref[h], qn, nt, preferred_element_type=F32))
        past = row < qb + 1
        gm = jnp.where(past, gate, -jnp.inf)
        rank = jnp.zeros((km_rows, blk), jnp.int32)
        for j in range(n_blocks - 1):
            gj = gm[j:j + 1, :]
            beats = (gj > gm) | ((gj == gm) & (row > j))
            rank = rank + beats.astype(jnp.int32)
        sel = past & (rank < MOBA_TOPK)
        bias_t = jnp.where(sel, 0.0, MASK_VALUE)
        bias_t = jnp.concatenate([bias_t, jnp.zeros((dh - km_rows, blk), F32)], axis=0)
        bias_ref[1 - slot, h] = bias_t.T.astype(BF16)

    def attend(nb):
        for h in range(hp):
            cols = slice(h * dh, (h + 1) * dh)
            q = q_ref[:, cols]
            s_own = lax.dot_general(q, kaug_ref[h, pl.ds(start, blk), :dh], nt, preferred_element_type=F32)
            s_own = jnp.where(causal, s_own, MASK_VALUE)
            m = jnp.max(s_own, axis=-1, keepdims=True)
            v_own = vaug_ref[h, pl.ds(start, blk), :]
            if nb == 0:
                p_own = jnp.exp2((s_own - m).astype(BF16))
                acc = jnp.dot(p_own, v_own, preferred_element_type=F32)
            else:
                q_aug = jnp.concatenate([q, bias_ref[slot, h]], axis=1)
                s = lax.dot_general(q_aug, kaug_ref[h, :nb * blk, :], nt, preferred_element_type=F32)
                m = jnp.maximum(m, jnp.max(s, axis=-1, keepdims=True))
                p_own = jnp.exp2((s_own - m).astype(BF16))
                p = jnp.exp2((s - m).astype(BF16))
                acc = (jnp.dot(p_own, v_own, preferred_element_type=F32)
                       + jnp.dot(p, vaug_ref[h, :nb * blk, :], preferred_element_type=F32))
            o_ref[:, cols] = (acc[:, :dh] / acc[:, dh:dh + 1]).astype(o_ref.dtype)
        for h in range(hp):
            select_next(h)

    n_cls = -(-(n_blocks - 1) // group)
    cls = (qb + (group - 1)) // group
    for c in range(n_cls + 1):
        pl.when(cls == c)(functools.partial(attend, min(c * group, n_blocks)))


def _moba(q, k, v, *, batch, seq, heads, hp=MOBA_HEADS_PER_STEP):
    m = batch * seq
    blk, dh = MOBA_BLOCK, MOBA_HEAD_DIM
    n_blocks = seq // blk
    km_rows = -(-n_blocks // BF16_SUBLANES) * BF16_SUBLANES
    return pl.pallas_call(
        functools.partial(_moba_body, n_blocks=n_blocks, group=MOBA_KEY_GROUP, hp=hp),
        grid=(batch, heads // hp, n_blocks),
        in_specs=[pl.BlockSpec((blk, hp * dh), lambda b, h, t: (b * n_blocks + t, h)),
                  pl.BlockSpec((blk, hp * dh),
                               lambda b, h, t: (b * n_blocks + jnp.minimum(t + 1, n_blocks - 1), h)),
                  pl.BlockSpec((seq, hp * dh), lambda b, h, t: (b, h)),
                  pl.BlockSpec((seq, hp * dh), lambda b, h, t: (b, h))],
        out_specs=pl.BlockSpec((blk, hp * dh), lambda b, h, t: (b * n_blocks + t, h)),
        out_shape=jax.ShapeDtypeStruct((m, heads * dh), BF16),
        scratch_shapes=[pltpu.VMEM((hp, seq, 2 * dh), BF16),
                        pltpu.VMEM((hp, seq, 2 * dh), BF16),
                        pltpu.VMEM((hp, km_rows, dh), BF16),
                        pltpu.VMEM((hp, km_rows, dh), BF16),
                        pltpu.VMEM((2, hp, blk, dh), BF16)],
        compiler_params=_params("parallel", "parallel", "arbitrary"),
        name="moba",
    )(q, q, k, v)


def _mix_out_body(a1_ref, a2_ref, w_ref, r_ref, o_ref):
    half = a1_ref.shape[1]
    acc = jnp.dot(a1_ref[...], w_ref[:half, :], preferred_element_type=F32)
    acc = acc + jnp.dot(a2_ref[...], w_ref[half:, :], preferred_element_type=F32)
    o_ref[...] = acc + r_ref[...]


def _mix_out(a1, a2, w, res, tm=1024, tn=1024):
    m, half = a1.shape
    n = w.shape[1]
    tm, tn = min(tm, m), min(tn, n)
    return pl.pallas_call(
        _mix_out_body,
        grid=(m // tm, n // tn),
        in_specs=[pl.BlockSpec((tm, half), lambda i, j: (i, 0)),
                  pl.BlockSpec((tm, half), lambda i, j: (i, 0)),
                  pl.BlockSpec((2 * half, tn), lambda i, j: (0, j)),
                  pl.BlockSpec((tm, tn), lambda i, j: (i, j))],
        out_specs=pl.BlockSpec((tm, tn), lambda i, j: (i, j)),
        out_shape=jax.ShapeDtypeStruct((m, n), F32),
        compiler_params=_params("parallel", "parallel"),
        name="mix_out",
    )(a1, a2, w, res)


def _cross_body(q_ref, k_ref, v_ref, o_ref, *, heads):
    d = q_ref.shape[1]
    dh = d // heads
    scale = dh ** -0.5
    for h in range(heads):
        sl = slice(h * dh, (h + 1) * dh)
        s = lax.dot_general(q_ref[:, sl], k_ref[:, sl], (((1,), (1,)), ((), ())),
                            preferred_element_type=F32) * scale
        s = s - jnp.max(s, axis=-1, keepdims=True)
        e = jnp.exp(s)
        p = e / jnp.sum(e, axis=-1, keepdims=True)
        o_ref[:, sl] = jnp.dot(p.astype(BF16), v_ref[:, sl],
                               preferred_element_type=F32).astype(o_ref.dtype)


def _cross(q, k, v, *, batch, seq, n_mem, heads, rows=512):
    m, d = q.shape
    rows = min(rows, seq)
    steps = seq // rows
    return pl.pallas_call(
        functools.partial(_cross_body, heads=heads),
        grid=(batch, steps),
        in_specs=[pl.BlockSpec((rows, d), lambda b, t: (b * steps + t, 0)),
                  pl.BlockSpec((n_mem, d), lambda b, t: (b, 0)),
                  pl.BlockSpec((n_mem, d), lambda b, t: (b, 0))],
        out_specs=pl.BlockSpec((rows, d), lambda b, t: (b * steps + t, 0)),
        out_shape=jax.ShapeDtypeStruct((m, d), BF16),
        compiler_params=_params("parallel", "parallel"),
        name="cross_attn",
    )(q, k, v)


def _layer(h, mem2, batch, seq, n_mem, norm_mix_g, w_in, w_gate_up, b_gate, gla_norm_g, w_out,
           norm_cross_g, norm_mem_g, w_cq, w_ck, w_cv, w_co, norm_mlp_g, w_up, w_down):
    m, d = h.shape
    gla_dk = GLA_HEADS * GLA_HEAD_K
    gla_dv = GLA_HEADS * GLA_HEAD_V
    moba_w = d - gla_dv
    moba_heads = moba_w // MOBA_HEAD_DIM
    o_glow = 2 * gla_dk + gla_dv
    o_gout = o_glow + GLA_GATE_RANK
    in_width = w_in.shape[1]
    w_gu = jnp.pad(w_gate_up, ((0, LANES - GLA_GATE_RANK), (0, 0)))
    tile = 1024
    row_tiles = m // tile

    norm_rows = 256
    xn, w_gla = _rmsnorm(h, norm_mix_g, BF16, rows=norm_rows,
                         side=(_row_blocks(w_in, m // norm_rows), o_glow))
    steps = row_tiles * (o_glow // tile)
    gla_qkv, (w_rest, w_glow, w_out_b) = _matmul(
        xn, w_gla, out_dtype=BF16, name="proj_gla_qkv",
        sides=[(_row_blocks(w_in, steps), ((o_gout, in_width, in_width - o_gout), (o_glow, o_gout, LANES))),
               (_row_blocks(w_out, steps), None)])
    c_mq = gla_dv
    c_mk = c_mq + moba_w
    c_mv = c_mk + moba_w
    g = _gate(xn, w_glow, w_gu, b_gate)
    steps = row_tiles * (gla_dv // tile)
    g_out, (w_cq_b,) = _matmul(xn, w_rest, col0=0, n=gla_dv, out_dtype=BF16, name="proj_gla_gout",
                               sides=[(_row_blocks(w_cq, steps), None)])
    o_gla = _gla(gla_qkv, g, g_out, gla_norm_g, batch=batch, seq=seq, heads=GLA_HEADS)

    cos_t, sin_t = _rotary_tables(seq)
    steps = row_tiles * (moba_w // tile)
    mq, (w_ck_b,) = _matmul(xn, w_rest, col0=c_mq, n=moba_w, out_dtype=BF16, name="proj_moba_q",
                            rot=(cos_t, sin_t, MOBA_HEAD_DIM ** -0.5 * LOG2E),
                            sides=[(_row_blocks(w_ck, steps), None)])
    mk, (w_cv_b,) = _matmul(xn, w_rest, col0=c_mk, n=moba_w, out_dtype=BF16, name="proj_moba_k",
                            rot=(cos_t, sin_t, 1.0), sides=[(_row_blocks(w_cv, steps), None)])
    mv, (w_co_b,) = _matmul(xn, w_rest, col0=c_mv, n=moba_w, out_dtype=BF16, name="proj_moba_v",
                            sides=[(_row_blocks(w_co, steps), None)])
    o_moba = _moba(mq, mk, mv, batch=batch, seq=seq, heads=moba_heads)

    h = _mix_out(o_gla, o_moba, w_out_b, h)

    hn = _rmsnorm(h, norm_cross_g, BF16)
    mem_n = _rmsnorm(mem2, norm_mem_g, BF16)
    steps = row_tiles * (d // tile)
    cq, (w_up_b,) = _matmul(hn, w_cq_b, out_dtype=BF16, name="proj_cross_q",
                            sides=[(_row_blocks(w_up, steps), None)])
    ck = _matmul(mem_n, w_ck_b, out_dtype=BF16, name="proj_cross_k")
    cv = _matmul(mem_n, w_cv_b, out_dtype=BF16, name="proj_cross_v")
    o_cross = _cross(cq, ck, cv, batch=batch, seq=seq, n_mem=n_mem, heads=CROSS_HEADS)
    h = _matmul(o_cross, w_co_b, out_dtype=F32, res=h, name="proj_cross_out")

    hn = _rmsnorm(h, norm_mlp_g, BF16)
    steps = row_tiles * (w_up.shape[1] // tile)
    u, (w_down_b,) = _matmul(hn, w_up_b, out_dtype=BF16, act="relu2", name="mlp_up",
                             sides=[(_row_blocks(w_down, steps), None)])
    h = _matmul(u, w_down_b, out_dtype=F32, res=h, tk=2048, name="mlp_down")
    return h


def kernel(x, mem, norm_mix_g, w_in, w_gate_up, b_gate, gla_norm_g, w_out, norm_cross_g, norm_mem_g,
           w_cq, w_ck, w_cv, w_co, norm_mlp_g, w_up, w_down, norm_final_g):
    batch, seq, d = x.shape
    n_mem = mem.shape[1]
    h = x.reshape(batch * seq, d)
    mem2 = mem.reshape(batch * n_mem, d)
    for l in range(norm_mix_g.shape[0]):
        h = _layer(h, mem2, batch, seq, n_mem, norm_mix_g[l], w_in[l], w_gate_up[l], b_gate[l],
                   gla_norm_g[l], w_out[l], norm_cross_g[l], norm_mem_g[l], w_cq[l], w_ck[l],
                   w_cv[l], w_co[l], norm_mlp_g[l], w_up[l], w_down[l])
    out = _rmsnorm(h, norm_final_g, x.dtype)
    return out.reshape(batch, seq, d)
```

```python
import functools

import jax
import jax.numpy as jnp
from jax import lax
from jax.experimental import pallas as pl
from jax.experimental.pallas import tpu as pltpu

F32 = jnp.float32
BF16 = jnp.bfloat16

EPS = 1e-6
LANES = 128
BF16_SUBLANES = 16
VMEM_LIMIT_BYTES = 56 * 1024 * 1024

GLA_HEADS = 8
GLA_HEAD_K = 128
GLA_HEAD_V = 256
GLA_GATE_RANK = 16
GLA_GATE_NORM = 16.0
GLA_CHUNK = 64
MOBA_HEAD_DIM = 128
MOBA_BLOCK = 256
MOBA_TOPK = 3
MOBA_KEY_GROUP = 4
MOBA_HEADS_PER_STEP = 4
ROPE_THETA = 500000.0
ROPE_DIM = MOBA_HEAD_DIM // 4
ROPE_HALF = ROPE_DIM // 2
CROSS_HEADS = 4
MASK_VALUE = -1e30
LOG2E = 1.4426950408889634


def _split_bf16(x, terms):
    parts = []
    for _ in range(terms - 1):
        p = x.astype(BF16)
        parts.append(p)
        x = x - p.astype(F32)
    parts.append(x.astype(BF16))
    return parts


def _params(*sem):
    return pltpu.CompilerParams(dimension_semantics=sem, vmem_limit_bytes=VMEM_LIMIT_BYTES)


def _rmsnorm_body(x_ref, g_ref, *rest):
    o_ref = rest[-1] if len(rest) == 1 else rest[1]
    x = x_ref[...].astype(F32)
    ms = jnp.mean(x * x, axis=-1, keepdims=True)
    y = x * lax.rsqrt(ms + EPS)
    o_ref[...] = (y * g_ref[...]).astype(o_ref.dtype)
    if len(rest) == 3:
        rest[2][...] = rest[0][...].astype(BF16)


def _rmsnorm(x, g, out_dtype, rows=256, side=None):
    m, d = x.shape
    rows = min(rows, m)
    steps = m // rows
    in_specs = [pl.BlockSpec((rows, d), lambda i: (i, 0)),
                pl.BlockSpec((1, d), lambda i: (0, 0))]
    out_specs = [pl.BlockSpec((rows, d), lambda i: (i, 0))]
    out_shape = [jax.ShapeDtypeStruct((m, d), out_dtype)]
    args = [x, g.reshape(1, d).astype(F32)]
    if side is not None:
        src, n_rows = side
        cols = src.shape[1]
        assert n_rows % steps == 0 and (n_rows // steps) % BF16_SUBLANES == 0 and src.dtype == F32
        r = n_rows // steps
        in_specs.append(pl.BlockSpec((r, cols), lambda i: (i, 0)))
        out_specs.append(pl.BlockSpec((r, cols), lambda i: (i, 0)))
        out_shape.append(jax.ShapeDtypeStruct((n_rows, cols), BF16))
        args.append(src)
    outs = pl.pallas_call(
        _rmsnorm_body,
        grid=(steps,),
        in_specs=in_specs,
        out_specs=out_specs,
        out_shape=out_shape,
        compiler_params=_params("parallel"),
        name="rmsnorm",
    )(*args)
    return outs[0] if side is None else tuple(outs)


def _mm_body(*refs, nk, act, has_res, rot_scale, n_sides, w_t):
    it = iter(refs)
    a_ref, w_ref = next(it), next(it)
    r_ref = next(it) if has_res else None
    cos_ref, sin_ref = (next(it), next(it)) if rot_scale is not None else (None, None)
    side_in = [next(it) for _ in range(n_sides)]
    o_ref = next(it)
    side_out = [next(it) for _ in range(n_sides)]
    acc_ref = next(it) if nk > 1 else None

    for src_ref, dst_ref in zip(side_in, side_out):
        dst_ref[...] = src_ref[...].astype(BF16)

    def product():
        if w_t:
            return lax.dot_general(a_ref[...], w_ref[...], (((1,), (1,)), ((), ())),
                                   preferred_element_type=F32)
        return jnp.dot(a_ref[...], w_ref[...], preferred_element_type=F32)

    def finish(acc):
        if act == "relu2":
            acc = jnp.square(jnp.maximum(acc, 0.0))
        if has_res:
            acc = acc + r_ref[...]
        if rot_scale is None:
            o_ref[...] = acc.astype(o_ref.dtype)
            return
        tm, tn = acc.shape
        cos = cos_ref[...]
        sin = sin_ref[...]
        lane = lax.broadcasted_iota(jnp.int32, (tm, LANES), 1)
        for c in range(tn // LANES):
            xs = acc[:, c * LANES:(c + 1) * LANES]
            partner = jnp.where(lane < ROPE_HALF,
                                pltpu.roll(xs, LANES - ROPE_HALF, 1),
                                pltpu.roll(xs, ROPE_HALF, 1))
            y = xs * cos + partner * sin
            if rot_scale != 1.0:
                y = y * rot_scale
            o_ref[:, c * LANES:(c + 1) * LANES] = y.astype(o_ref.dtype)

    if nk == 1:
        finish(product())
        return

    k = pl.program_id(2)

    @pl.when(k == 0)
    def _():
        acc_ref[...] = product()

    @pl.when(k > 0)
    def _():
        acc_ref[...] += product()

    @pl.when(k == nk - 1)
    def _():
        finish(acc_ref[...])


def _matmul(a, w, *, out_dtype, res=None, act=None, rot=None, w_t=False, col0=0, n=None, sides=(),
            tm=1024, tn=1024, tk=4096, name="matmul"):
    m, kdim = a.shape
    n_total = w.shape[0] if w_t else w.shape[1]
    n = n_total if n is None else n
    tm, tn, tk = min(tm, m), min(tn, n), min(tk, kdim)
    if rot is not None:
        tm = min(tm, rot[0].shape[0])
    nk = kdim // tk
    assert col0 % tn == 0 and n % tn == 0 and m % tm == 0 and kdim % tk == 0
    j0 = col0 // tn
    nj = n // tn
    steps = (m // tm) * nj * nk
    w_spec = (pl.BlockSpec((tn, tk), lambda i, j, k: (j0 + j, k)) if w_t
              else pl.BlockSpec((tk, tn), lambda i, j, k: (k, j0 + j)))
    in_specs = [pl.BlockSpec((tm, tk), lambda i, j, k: (i, k)), w_spec]
    args = [a, w]
    if res is not None:
        in_specs.append(pl.BlockSpec((tm, tn), lambda i, j, k: (i, j)))
        args.append(res)
    if rot is not None:
        cos_t, sin_t, rot_scale = rot
        assert nk == 1 and cos_t.shape[0] % tm == 0
        tiles_per_seq = cos_t.shape[0] // tm
        in_specs += [pl.BlockSpec((tm, LANES), lambda i, j, k: (i % tiles_per_seq, 0))] * 2
        args += [cos_t, sin_t]
    else:
        rot_scale = None
    step = lambda i, j, k: (i * nj + j) * nk + k
    out_specs = [pl.BlockSpec((tm, tn), lambda i, j, k: (i, j))]
    out_shape = [jax.ShapeDtypeStruct((m, n), out_dtype)]
    for kind, src, *window in sides:
        assert src.dtype == F32 and src.ndim == 2
        cols = src.shape[1]
        r0, nr = window if window else (0, src.shape[0])
        assert r0 % BF16_SUBLANES == 0 and nr % BF16_SUBLANES == 0
        if kind == "fixed":
            assert r0 % nr == 0
            in_specs.append(pl.BlockSpec((nr, cols), lambda i, j, k, b=r0 // nr: (b, 0)))
            out_specs.append(pl.BlockSpec((nr, cols), lambda i, j, k: (0, 0)))
        else:
            assert nr % steps == 0 and (nr // steps) % BF16_SUBLANES == 0
            r = nr // steps
            if r0 % r == 0:
                in_specs.append(pl.BlockSpec((r, cols), lambda i, j, k, b=r0 // r: (b + step(i, j, k), 0)))
            else:
                in_specs.append(pl.BlockSpec((pl.Element(r), pl.Element(cols)),
                                             lambda i, j, k, r0=r0, r=r: (
                                                 pl.multiple_of(r0 + step(i, j, k) * r, BF16_SUBLANES), 0)))
            out_specs.append(pl.BlockSpec((r, cols), lambda i, j, k: (step(i, j, k), 0)))
        out_shape.append(jax.ShapeDtypeStruct((nr, cols), BF16))
        args.append(src)
    scratch = [pltpu.VMEM((tm, tn), F32)] if nk > 1 else []
    outs = pl.pallas_call(
        functools.partial(_mm_body, nk=nk, act=act, has_res=res is not None, rot_scale=rot_scale,
                          n_sides=len(sides), w_t=w_t),
        grid=(m // tm, nj, nk),
        in_specs=in_specs,
        out_specs=out_specs,
        out_shape=out_shape,
        scratch_shapes=scratch,
        compiler_params=_params("parallel", "parallel", "arbitrary"),
        name=name,
    )(*args)
    if not sides:
        return outs[0]
    return outs[0], list(outs[1:])


def _rotary_tables(seq):
    inv_freq = ROPE_THETA ** (-jnp.arange(0, ROPE_DIM, 2, dtype=F32) / ROPE_DIM)
    ang = jnp.arange(seq).astype(F32)[:, None] * inv_freq[None, :]
    cos, sin = jnp.cos(ang), jnp.sin(ang)
    pad = MOBA_HEAD_DIM - ROPE_DIM
    cos_t = jnp.concatenate([cos, cos, jnp.ones((seq, pad), F32)], axis=-1)
    sin_t = jnp.concatenate([-sin, sin, jnp.zeros((seq, pad), F32)], axis=-1)
    return cos_t, sin_t


def _gate_body(a_ref, wl_ref, wu_ref, b_ref, o_ref):
    g_low = lax.dot_general(a_ref[...], wl_ref[...], (((1,), (1,)), ((), ())),
                            preferred_element_type=F32)
    g_hi, g_lo = _split_bf16(g_low, 2)
    w_hi, w_lo = _split_bf16(wu_ref[...], 2)
    z = (jnp.dot(g_hi, w_hi, preferred_element_type=F32) + jnp.dot(g_lo, w_hi, preferred_element_type=F32)
         + jnp.dot(g_hi, w_lo, preferred_element_type=F32))
    z = z + b_ref[...]
    log_sig = jnp.minimum(z, 0.0) - jnp.log1p(jnp.exp(-jnp.abs(z)))
    o_ref[...] = log_sig * (1.0 / GLA_GATE_NORM)


def _gate(xn, w_low, w_up_pad, b, rows=1024):
    m, d = xn.shape
    r = w_low.shape[0]
    n = w_up_pad.shape[1]
    rows = min(rows, m)
    return pl.pallas_call(
        _gate_body,
        grid=(m // rows,),
        in_specs=[pl.BlockSpec((rows, d), lambda i: (i, 0)),
                  pl.BlockSpec((r, d), lambda i: (0, 0)),
                  pl.BlockSpec((r, n), lambda i: (0, 0)),
                  pl.BlockSpec((1, n), lambda i: (0, 0))],
        out_specs=pl.BlockSpec((rows, n), lambda i: (i, 0)),
        out_shape=jax.ShapeDtypeStruct((m, n), F32),
        compiler_params=_params("parallel"),
        name="gla_gate",
    )(xn, w_low, w_up_pad, b.reshape(1, n).astype(F32))


def _gla_body(q_ref, k_ref, v_ref, g_ref, go_ref, gn_ref, o_ref, st_ref, *, chunks):
    c_len = GLA_CHUNK

    @pl.when(pl.program_id(2) == 0)
    def _():
        st_ref[...] = jnp.zeros_like(st_ref)

    row = lax.broadcasted_iota(jnp.int32, (c_len, c_len), 0)
    col = lax.broadcasted_iota(jnp.int32, (c_len, c_len), 1)
    causal = col <= row
    tri = causal.astype(BF16)
    q_scale = GLA_HEAD_K ** -0.5
    gn = gn_ref[...]

    nt = (((1,), (1,)), ((), ()))
    tn = (((0,), (0,)), ((), ()))

    dk = g_ref.shape[1]
    g_all = jnp.concatenate([g_ref[pl.ds(c * c_len, c_len), :] for c in range(chunks)], axis=1)
    b_all = sum(jnp.dot(tri, part, preferred_element_type=F32) for part in _split_bf16(g_all, 3))
    b_last_all = b_all[c_len - 1:c_len, :]
    e_pos = jnp.exp(b_all)
    e_neg = jnp.exp(-b_all)
    e_end = jnp.exp(b_last_all - b_all)
    decay_all = jnp.exp(b_last_all)

    q_decs, o_intras, d_states, decays = [], [], [], []
    for c in range(chunks):
        sl = pl.ds(c * c_len, c_len)
        lanes = slice(c * dk, (c + 1) * dk)
        q = q_ref[sl, :].astype(F32) * q_scale
        k = k_ref[sl, :].astype(F32)
        v = v_ref[sl, :]
        q_dec = (q * e_pos[:, lanes]).astype(BF16)
        k_inv = (k * e_neg[:, lanes]).astype(BF16)
        k_end = (k * e_end[:, lanes]).astype(BF16)
        a = lax.dot_general(q_dec, k_inv, nt, preferred_element_type=F32)
        a = jnp.where(causal, a, 0.0).astype(BF16)
        q_decs.append(q_dec)
        o_intras.append(jnp.dot(a, v, preferred_element_type=F32))
        d_states.append(lax.dot_general(v, k_end, tn, preferred_element_type=F32))
        decays.append(decay_all[:, lanes])

    st = st_ref[...]
    states = []
    for c in range(chunks):
        states.append(st.astype(BF16))
        st = st * decays[c] + d_states[c]
    st_ref[...] = st

    for c in range(chunks):
        sl = pl.ds(c * c_len, c_len)
        o = o_intras[c] + lax.dot_general(q_decs[c], states[c], nt, preferred_element_type=F32)
        ms = jnp.mean(o * o, axis=-1, keepdims=True)
        y = o * lax.rsqrt(ms + EPS) * gn
        go = go_ref[sl, :].astype(F32)
        y = y * (go * jax.nn.sigmoid(go))
        o_ref[sl, :] = y.astype(o_ref.dtype)


def _gla(qkv, g, g_out, gn, *, batch, seq, heads, rows=1024):
    m = batch * seq
    rows = min(rows, seq)
    steps = seq // rows
    hk, hv = GLA_HEAD_K, GLA_HEAD_V
    k_off = heads
    v_off = heads * 2 * hk // hv
    row_map = lambda b, h, t: b * steps + t
    return pl.pallas_call(
        functools.partial(_gla_body, chunks=rows // GLA_CHUNK),
        grid=(batch, heads, steps),
        in_specs=[pl.BlockSpec((rows, hk), lambda b, h, t: (row_map(b, h, t), h)),
                  pl.BlockSpec((rows, hk), lambda b, h, t: (row_map(b, h, t), k_off + h)),
                  pl.BlockSpec((rows, hv), lambda b, h, t: (row_map(b, h, t), v_off + h)),
                  pl.BlockSpec((rows, hk), lambda b, h, t: (row_map(b, h, t), h)),
                  pl.BlockSpec((rows, hv), lambda b, h, t: (row_map(b, h, t), h)),
                  pl.BlockSpec((1, hv), lambda b, h, t: (0, 0))],
        out_specs=pl.BlockSpec((rows, hv), lambda b, h, t: (row_map(b, h, t), h)),
        out_shape=jax.ShapeDtypeStruct((m, heads * hv), BF16),
        scratch_shapes=[pltpu.VMEM((hv, hk), F32)],
        compiler_params=_params("parallel", "parallel", "arbitrary"),
        name="gla",
    )(qkv, qkv, qkv, g, g_out, gn.reshape(1, hv).astype(F32))


def _moba_body(q_ref, qn_ref, k_ref, v_ref, o_ref, kaug_ref, vaug_ref, kmh_ref, kml_ref, bias_ref,
               *, n_blocks, group, hp):
    blk = MOBA_BLOCK
    dh = MOBA_HEAD_DIM
    seq = n_blocks * blk
    km_rows = kmh_ref.shape[1]
    qb = pl.program_id(2)
    nt = (((1,), (1,)), ((), ()))

    @pl.when(qb == 0)
    def _():
        r = lax.broadcasted_iota(jnp.int32, (seq, dh), 0)
        l = lax.broadcasted_iota(jnp.int32, (seq, dh), 1)
        one_hot = ((l * blk <= r) & (r < (l + 1) * blk)).astype(BF16)
        ones = jnp.ones((seq, dh), BF16)
        j = lax.broadcasted_iota(jnp.int32, (km_rows, seq), 0)
        s = lax.broadcasted_iota(jnp.int32, (km_rows, seq), 1)
        avg = jnp.where((j * blk <= s) & (s < (j + 1) * blk), 1.0 / blk, 0.0).astype(BF16)
        for h in range(hp):
            k = k_ref[:, h * dh:(h + 1) * dh]
            kaug_ref[h, :, :dh] = k
            kaug_ref[h, :, dh:] = one_hot
            vaug_ref[h, :, :dh] = v_ref[:, h * dh:(h + 1) * dh]
            vaug_ref[h, :, dh:] = ones
            km = jnp.dot(avg, k, preferred_element_type=F32)
            hi = km.astype(BF16)
            kmh_ref[h] = hi
            kml_ref[h] = (km - hi.astype(F32)).astype(BF16)

    row = lax.broadcasted_iota(jnp.int32, (km_rows, blk), 0)
    qi = lax.broadcasted_iota(jnp.int32, (blk, blk), 0)
    ki = lax.broadcasted_iota(jnp.int32, (blk, blk), 1)
    causal = ki <= qi
    start = pl.multiple_of(qb * blk, blk)
    slot = qb % 2

    def select_next(h):
        qn = qn_ref[:, h * dh:(h + 1) * dh]
        gate = (lax.dot_general(kmh_ref[h], qn, nt, preferred_element_type=F32)
                + lax.dot_general(kml_ref[h], qn, nt, preferred_element_type=F32))
        past = row < qb + 1
        gm = jnp.where(past, gate, -jnp.inf)
        rank = jnp.zeros((km_rows, blk), jnp.int32)
        for j in range(n_blocks - 1):
            gj = gm[j:j + 1, :]
            beats = (gj > gm) | ((gj == gm) & (row > j))
            rank = rank + beats.astype(jnp.int32)
        sel = past & (rank < MOBA_TOPK)
        bias_t = jnp.where(sel, 0.0, MASK_VALUE)
        bias_t = jnp.concatenate([bias_t, jnp.zeros((dh - km_rows, blk), F32)], axis=0)
        bias_ref[1 - slot, h] = bias_t.T.astype(BF16)

    def attend(nb):
        for h in range(hp):
            cols = slice(h * dh, (h + 1) * dh)
            q = q_ref[:, cols]
            s_own = lax.dot_general(q, kaug_ref[h, pl.ds(start, blk), :dh], nt, preferred_element_type=F32)
            s_own = jnp.where(causal, s_own, MASK_VALUE)
            m = jnp.max(s_own, axis=-1, keepdims=True)
            v_own = vaug_ref[h, pl.ds(start, blk), :]
            if nb == 0:
                p_own = jnp.exp2((s_own - m).astype(BF16))
                acc = jnp.dot(p_own, v_own, preferred_element_type=F32)
            else:
                q_aug = jnp.concatenate([q, bias_ref[slot, h]], axis=1)
                s = lax.dot_general(q_aug, kaug_ref[h, :nb * blk, :], nt, preferred_element_type=F32)
                m = jnp.maximum(m, jnp.max(s, axis=-1, keepdims=True))
                p_own = jnp.exp2((s_own - m).astype(BF16))
                p = jnp.exp2((s - m).astype(BF16))
                acc = (jnp.dot(p_own, v_own, preferred_element_type=F32)
                       + jnp.dot(p, vaug_ref[h, :nb * blk, :], preferred_element_type=F32))
            o_ref[:, cols] = (acc[:, :dh] / acc[:, dh:dh + 1]).astype(o_ref.dtype)
        for h in range(hp):
            select_next(h)

    n_cls = -(-(n_blocks - 1) // group)
    cls = (qb + (group - 1)) // group
    for c in range(n_cls + 1):
        pl.when(cls == c)(functools.partial(attend, min(c * group, n_blocks)))


def _moba(q, k, v, *, batch, seq, heads, hp=MOBA_HEADS_PER_STEP):
    m = batch * seq
    blk, dh = MOBA_BLOCK, MOBA_HEAD_DIM
    n_blocks = seq // blk
    km_rows = -(-n_blocks // BF16_SUBLANES) * BF16_SUBLANES
    return pl.pallas_call(
        functools.partial(_moba_body, n_blocks=n_blocks, group=MOBA_KEY_GROUP, hp=hp),
        grid=(batch, heads // hp, n_blocks),
        in_specs=[pl.BlockSpec((blk, hp * dh), lambda b, h, t: (b * n_blocks + t, h)),
                  pl.BlockSpec((blk, hp * dh),
                               lambda b, h, t: (b * n_blocks + jnp.minimum(t + 1, n_blocks - 1), h)),
                  pl.BlockSpec((seq, hp * dh), lambda b, h, t: (b, h)),
                  pl.BlockSpec((seq, hp * dh), lambda b, h, t: (b, h))],
        out_specs=pl.BlockSpec((blk, hp * dh), lambda b, h, t: (b * n_blocks + t, h)),
        out_shape=jax.ShapeDtypeStruct((m, heads * dh), BF16),
        scratch_shapes=[pltpu.VMEM((hp, seq, 2 * dh), BF16),
                        pltpu.VMEM((hp, seq, 2 * dh), BF16),
                        pltpu.VMEM((hp, km_rows, dh), BF16),
                        pltpu.VMEM((hp, km_rows, dh), BF16),
                        pltpu.VMEM((2, hp, blk, dh), BF16)],
        compiler_params=_params("parallel", "parallel", "arbitrary"),
        name="moba",
    )(q, q, k, v)


def _mix_out_body(a1_ref, a2_ref, w_ref, r_ref, o_ref):
    half = a1_ref.shape[1]
    acc = jnp.dot(a1_ref[...], w_ref[:half, :], preferred_element_type=F32)
    acc = acc + jnp.dot(a2_ref[...], w_ref[half:, :], preferred_element_type=F32)
    o_ref[...] = acc + r_ref[...]


def _mix_out(a1, a2, w, res, tm=1024, tn=1024):
    m, half = a1.shape
    n = w.shape[1]
    tm, tn = min(tm, m), min(tn, n)
    return pl.pallas_call(
        _mix_out_body,
        grid=(m // tm, n // tn),
        in_specs=[pl.BlockSpec((tm, half), lambda i, j: (i, 0)),
                  pl.BlockSpec((tm, half), lambda i, j: (i, 0)),
                  pl.BlockSpec((2 * half, tn), lambda i, j: (0, j)),
                  pl.BlockSpec((tm, tn), lambda i, j: (i, j))],
        out_specs=pl.BlockSpec((tm, tn), lambda i, j: (i, j)),
        out_shape=jax.ShapeDtypeStruct((m, n), F32),
        compiler_params=_params("parallel", "parallel"),
        name="mix_out",
    )(a1, a2, w, res)


def _cross_body(q_ref, k_ref, v_ref, o_ref, *, heads):
    d = q_ref.shape[1]
    dh = d // heads
    scale = dh ** -0.5
    for h in range(heads):
        sl = slice(h * dh, (h + 1) * dh)
        s = lax.dot_general(q_ref[:, sl], k_ref[:, sl], (((1,), (1,)), ((), ())),
                            preferred_element_type=F32) * scale
        s = s - jnp.max(s, axis=-1, keepdims=True)
        e = jnp.exp(s)
        p = e / jnp.sum(e, axis=-1, keepdims=True)
        o_ref[:, sl] = jnp.dot(p.astype(BF16), v_ref[:, sl],
                               preferred_element_type=F32).astype(o_ref.dtype)


def _cross(q, k, v, *, batch, seq, n_mem, heads, rows=512):
    m, d = q.shape
    rows = min(rows, seq)
    steps = seq // rows
    return pl.pallas_call(
        functools.partial(_cross_body, heads=heads),
        grid=(batch, steps),
        in_specs=[pl.BlockSpec((rows, d), lambda b, t: (b * steps + t, 0)),
                  pl.BlockSpec((n_mem, d), lambda b, t: (b, 0)),
                  pl.BlockSpec((n_mem, d), lambda b, t: (b, 0))],
        out_specs=pl.BlockSpec((rows, d), lambda b, t: (b * steps + t, 0)),
        out_shape=jax.ShapeDtypeStruct((m, d), BF16),
        compiler_params=_params("parallel", "parallel"),
        name="cross_attn",
    )(q, k, v)


def _layer(h, mem2, batch, seq, n_mem, norm_mix_g, w_in, w_gate_up, b_gate, gla_norm_g, w_out,
           norm_cross_g, norm_mem_g, w_cq, w_ck, w_cv, w_co, norm_mlp_g, w_up, w_down):
    m, d = h.shape
    gla_dk = GLA_HEADS * GLA_HEAD_K
    gla_dv = GLA_HEADS * GLA_HEAD_V
    moba_w = d - gla_dv
    moba_heads = moba_w // MOBA_HEAD_DIM
    o_glow = 2 * gla_dk + gla_dv
    o_gout = o_glow + GLA_GATE_RANK
    in_width = w_in.shape[1]
    w_gu = jnp.pad(w_gate_up, ((0, LANES - GLA_GATE_RANK), (0, 0)))
    w_in_t = jnp.swapaxes(w_in, 0, 1)

    xn, w_gla_t = _rmsnorm(h, norm_mix_g, BF16, side=(w_in_t, o_glow))
    gla_qkv, (w_rest_t, w_glow_t, w_out_b) = _matmul(
        xn, w_gla_t, w_t=True, out_dtype=BF16, name="proj_gla_qkv",
        sides=[("rows", w_in_t, o_gout, in_width - o_gout), ("fixed", w_in_t, o_glow, GLA_GATE_RANK),
               ("all", w_out)])
    c_mq = gla_dv
    c_mk = c_mq + moba_w
    c_mv = c_mk + moba_w
    g = _gate(xn, jnp.pad(w_glow_t, ((0, LANES - GLA_GATE_RANK), (0, 0))), w_gu, b_gate)
    g_out, (w_cq_b,) = _matmul(xn, w_rest_t, w_t=True, col0=0, n=gla_dv, out_dtype=BF16,
                               name="proj_gla_gout", sides=[("all", w_cq)])
    o_gla = _gla(gla_qkv, g, g_out, gla_norm_g, batch=batch, seq=seq, heads=GLA_HEADS)

    cos_t, sin_t = _rotary_tables(seq)
    mq, (w_ck_b,) = _matmul(xn, w_rest_t, w_t=True, col0=c_mq, n=moba_w, out_dtype=BF16, name="proj_moba_q",
                            rot=(cos_t, sin_t, MOBA_HEAD_DIM ** -0.5 * LOG2E), sides=[("all", w_ck)])
    mk, (w_cv_b,) = _matmul(xn, w_rest_t, w_t=True, col0=c_mk, n=moba_w, out_dtype=BF16, name="proj_moba_k",
                            rot=(cos_t, sin_t, 1.0), sides=[("all", w_cv)])
    mv, (w_co_b,) = _matmul(xn, w_rest_t, w_t=True, col0=c_mv, n=moba_w, out_dtype=BF16, name="proj_moba_v",
                            sides=[("all", w_co)])
    o_moba = _moba(mq, mk, mv, batch=batch, seq=seq, heads=moba_heads)

    h = _mix_out(o_gla, o_moba, w_out_b, h)

    hn = _rmsnorm(h, norm_cross_g, BF16)
    mem_n = _rmsnorm(mem2, norm_mem_g, BF16)
    cq, (w_up_b,) = _matmul(hn, w_cq_b, out_dtype=BF16, name="proj_cross_q", sides=[("all", w_up)])
    ck = _matmul(mem_n, w_ck_b, out_dtype=BF16, name="proj_cross_k")
    cv = _matmul(mem_n, w_cv_b, out_dtype=BF16, name="proj_cross_v")
    o_cross = _cross(cq, ck, cv, batch=batch, seq=seq, n_mem=n_mem, heads=CROSS_HEADS)
    h = _matmul(o_cross, w_co_b, out_dtype=F32, res=h, name="proj_cross_out")

    hn = _rmsnorm(h, norm_mlp_g, BF16)
    u, (w_down_b,) = _matmul(hn, w_up_b, out_dtype=BF16, act="relu2", name="mlp_up", sides=[("all", w_down)])
    h = _matmul(u, w_down_b, out_dtype=F32, res=h, tk=2048, name="mlp_down")
    return h


def kernel(x, mem, norm_mix_g, w_in, w_gate_up, b_gate, gla_norm_g, w_out, norm_cross_g, norm_mem_g,
           w_cq, w_ck, w_cv, w_co, norm_mlp_g, w_up, w_down, norm_final_g):
    batch, seq, d = x.shape
    n_mem = mem.shape[1]
    h = x.reshape(batch * seq, d)
    mem2 = mem.reshape(batch * n_mem, d)
    for l in range(norm_mix_g.shape[0]):
        h = _layer(h, mem2, batch, seq, n_mem, norm_mix_g[l], w_in[l], w_gate_up[l], b_gate[l],
                   gla_norm_g[l], w_out[l], norm_cross_g[l], norm_mem_g[l], w_cq[l], w_ck[l],
                   w_cv[l], w_co[l], norm_mlp_g[l], w_up[l], w_down[l])
    out = _rmsnorm(h, norm_final_g, x.dtype)
    return out.reshape(batch, seq, d)
```

```python
import functools

import jax
import jax.numpy as jnp
from jax import lax
from jax.experimental import pallas as pl
from jax.experimental.pallas import tpu as pltpu

F32 = jnp.float32
BF16 = jnp.bfloat16

EPS = 1e-6
LANES = 128
BF16_SUBLANES = 16
VMEM_LIMIT_BYTES = 62 * 1024 * 1024

GLA_HEADS = 8
GLA_HEAD_K = 128
GLA_HEAD_V = 256
GLA_GATE_RANK = 16
GLA_GATE_NORM = 16.0
GLA_CHUNK = 64
MOBA_HEAD_DIM = 128
MOBA_BLOCK = 256
MOBA_TOPK = 3
MOBA_KEY_GROUP = 4
MOBA_HEADS_PER_STEP = 4
ROPE_THETA = 500000.0
ROPE_DIM = MOBA_HEAD_DIM // 4
ROPE_HALF = ROPE_DIM // 2
CROSS_HEADS = 4
MASK_VALUE = -1e30
LOG2E = 1.4426950408889634


def _split_bf16(x, terms):
    parts = []
    for _ in range(terms - 1):
        p = x.astype(BF16)
        parts.append(p)
        x = x - p.astype(F32)
    parts.append(x.astype(BF16))
    return parts


def _params(*sem):
    return pltpu.CompilerParams(dimension_semantics=sem, vmem_limit_bytes=VMEM_LIMIT_BYTES)


def _rmsnorm_body(x_ref, g_ref, *rest):
    o_ref = rest[-1] if len(rest) == 1 else rest[1]
    x = x_ref[...].astype(F32)
    ms = jnp.mean(x * x, axis=-1, keepdims=True)
    y = x * lax.rsqrt(ms + EPS)
    o_ref[...] = (y * g_ref[...]).astype(o_ref.dtype)
    if len(rest) == 3:
        rest[2][...] = rest[0][...].astype(BF16)


def _rmsnorm(x, g, out_dtype, rows=256, side=None):
    m, d = x.shape
    rows = min(rows, m)
    steps = m // rows
    in_specs = [pl.BlockSpec((rows, d), lambda i: (i, 0)),
                pl.BlockSpec((1, d), lambda i: (0, 0))]
    out_specs = [pl.BlockSpec((rows, d), lambda i: (i, 0))]
    out_shape = [jax.ShapeDtypeStruct((m, d), out_dtype)]
    args = [x, g.reshape(1, d).astype(F32)]
    if side is not None:
        src, n_rows = side
        cols = src.shape[1]
        assert n_rows % steps == 0 and (n_rows // steps) % BF16_SUBLANES == 0 and src.dtype == F32
        r = n_rows // steps
        in_specs.append(pl.BlockSpec((r, cols), lambda i: (i, 0)))
        out_specs.append(pl.BlockSpec((r, cols), lambda i: (i, 0)))
        out_shape.append(jax.ShapeDtypeStruct((n_rows, cols), BF16))
        args.append(src)
    outs = pl.pallas_call(
        _rmsnorm_body,
        grid=(steps,),
        in_specs=in_specs,
        out_specs=out_specs,
        out_shape=out_shape,
        compiler_params=_params("parallel"),
        name="rmsnorm",
    )(*args)
    return outs[0] if side is None else tuple(outs)


def _emit_norm_parts(h, hb_ref, ss_ref):
    hb_ref[...] = h.astype(BF16)
    ss_ref[...] = jnp.broadcast_to(jnp.sum(h * h, axis=1, keepdims=True), ss_ref.shape)


def _mm_body(*refs, nk, act, has_res, rot_scale, side_gains, w_t, emit_norm, norm_width):
    it = iter(refs)
    a_ref, w_ref = next(it), next(it)
    r_ref = next(it) if has_res else None
    cos_ref, sin_ref = (next(it), next(it)) if rot_scale is not None else (None, None)
    ss_in_ref = next(it) if norm_width else None
    side_in = [(next(it), next(it) if gain else None) for gain in side_gains]
    o_ref = next(it)
    hb_ref, ss_ref = (next(it), next(it)) if emit_norm else (None, None)
    side_out = [next(it) for _ in side_gains]
    acc_ref = next(it) if nk > 1 and o_ref.dtype != F32 else o_ref

    for (src_ref, gain_ref), dst_ref in zip(side_in, side_out):
        src = src_ref[...]
        if gain_ref is not None:
            src = src * gain_ref[...]
        dst_ref[...] = src.astype(BF16)

    def product(cols=None):
        w = w_ref[...] if cols is None else (w_ref[cols, :] if w_t else w_ref[:, cols])
        if w_t:
            return lax.dot_general(a_ref[...], w, (((1,), (1,)), ((), ())), preferred_element_type=F32)
        return jnp.dot(a_ref[...], w, preferred_element_type=F32)

    def finish(acc):
        if norm_width:
            ms = jnp.sum(ss_in_ref[...], axis=1, keepdims=True) * (1.0 / (LANES * norm_width))
            acc = acc * lax.rsqrt(ms + EPS)
        if act == "relu2":
            acc = jnp.square(jnp.maximum(acc, 0.0))
        if has_res:
            acc = acc + r_ref[...]
        o_ref[...] = acc.astype(o_ref.dtype)
        if emit_norm:
            _emit_norm_parts(acc, hb_ref, ss_ref)

    def finish_rotary():
        tm, tn = o_ref.shape
        cos = cos_ref[...]
        sin = sin_ref[...]
        lane = lax.broadcasted_iota(jnp.int32, (tm, LANES), 1)
        half = tn // 2
        for c0 in (0, half):
            acc = product(slice(c0, c0 + half))
            for c in range(half // LANES):
                xs = acc[:, c * LANES:(c + 1) * LANES]
                partner = jnp.where(lane < ROPE_HALF,
                                    pltpu.roll(xs, LANES - ROPE_HALF, 1),
                                    pltpu.roll(xs, ROPE_HALF, 1))
                y = xs * cos + partner * sin
                if rot_scale != 1.0:
                    y = y * rot_scale
                o_ref[:, c0 + c * LANES:c0 + (c + 1) * LANES] = y.astype(o_ref.dtype)

    if rot_scale is not None:
        finish_rotary()
        return
    if nk == 1:
        finish(product())
        return

    k = pl.program_id(2)

    @pl.when(k == 0)
    def _():
        acc_ref[...] = product()

    @pl.when(k > 0)
    def _():
        acc_ref[...] += product()

    @pl.when(k == nk - 1)
    def _():
        finish(acc_ref[...])


def _side_all(src, gain=None):
    return ("rows", src, 0, src.shape[0], gain)


def _side_rows(src, r0, nr):
    return ("rows", src, r0, nr, None)


def _side_fixed(src, r0, nr):
    return ("fixed", src, r0, nr, None)


def _matmul(a, w, *, out_dtype, res=None, act=None, rot=None, w_t=False, col0=0, n=None, sides=(),
            row_ss=None, emit_norm=False, tm=1024, tn=1024, tk=4096, name="matmul"):
    m, kdim = a.shape
    n_total = w.shape[0] if w_t else w.shape[1]
    n = n_total if n is None else n
    tm, tn, tk = min(tm, m), min(tn, n), min(tk, kdim)
    if rot is not None:
        tm = min(tm, rot[0].shape[0])
    nk = kdim // tk
    assert col0 % tn == 0 and n % tn == 0 and m % tm == 0 and kdim % tk == 0
    j0 = col0 // tn
    nj = n // tn
    steps = (m // tm) * nj * nk
    w_spec = (pl.BlockSpec((tn, tk), lambda i, j, k: (j0 + j, k)) if w_t
              else pl.BlockSpec((tk, tn), lambda i, j, k: (k, j0 + j)))
    in_specs = [pl.BlockSpec((tm, tk), lambda i, j, k: (i, k)), w_spec]
    args = [a, w]
    if res is not None:
        in_specs.append(pl.BlockSpec((tm, tn), lambda i, j, k: (i, j)))
        args.append(res)
    if rot is not None:
        cos_t, sin_t, rot_scale = rot
        assert nk == 1 and cos_t.shape[0] % tm == 0 and (tn // 2) % LANES == 0
        tiles_per_seq = cos_t.shape[0] // tm
        in_specs += [pl.BlockSpec((tm, LANES), lambda i, j, k: (i % tiles_per_seq, 0))] * 2
        args += [cos_t, sin_t]
    else:
        rot_scale = None
    if row_ss is not None:
        assert nk == 1 and row_ss.shape[0] == m
        in_specs.append(pl.BlockSpec((tm, row_ss.shape[1]), lambda i, j, k: (i, 0)))
        args.append(row_ss)
    step = lambda i, j, k: (i * nj + j) * nk + k
    out_specs = [pl.BlockSpec((tm, tn), lambda i, j, k: (i, j))]
    out_shape = [jax.ShapeDtypeStruct((m, n), out_dtype)]
    if emit_norm:
        assert out_dtype == F32
        out_specs += [pl.BlockSpec((tm, tn), lambda i, j, k: (i, j)),
                      pl.BlockSpec((tm, LANES), lambda i, j, k: (i, j))]
        out_shape += [jax.ShapeDtypeStruct((m, n), BF16), jax.ShapeDtypeStruct((m, nj * LANES), F32)]
    side_gains = []
    for kind, src, r0, nr, gain in sides:
        assert src.dtype == F32 and src.ndim == 2
        cols = src.shape[1]
        assert r0 % BF16_SUBLANES == 0 and nr % BF16_SUBLANES == 0
        if kind == "fixed":
            assert r0 % nr == 0 and gain is None
            in_specs.append(pl.BlockSpec((nr, cols), lambda i, j, k, b=r0 // nr: (b, 0)))
            out_specs.append(pl.BlockSpec((nr, cols), lambda i, j, k: (0, 0)))
            args.append(src)
        else:
            assert nr % steps == 0 and (nr // steps) % BF16_SUBLANES == 0
            r = nr // steps
            if r0 % r == 0:
                in_specs.append(pl.BlockSpec((r, cols), lambda i, j, k, b=r0 // r: (b + step(i, j, k), 0)))
            else:
                in_specs.append(pl.BlockSpec((pl.Element(r), pl.Element(cols)),
                                             lambda i, j, k, r0=r0, r=r: (
                                                 pl.multiple_of(r0 + step(i, j, k) * r, BF16_SUBLANES), 0)))
            args.append(src)
            if gain is not None:
                assert r0 == 0 and gain.shape == (nr,)
                in_specs.append(pl.BlockSpec((r, 1), lambda i, j, k: (step(i, j, k), 0)))
                args.append(gain.reshape(nr, 1).astype(F32))
            out_specs.append(pl.BlockSpec((r, cols), lambda i, j, k: (step(i, j, k), 0)))
        out_shape.append(jax.ShapeDtypeStruct((nr, cols), BF16))
        side_gains.append(gain is not None)
    scratch = [pltpu.VMEM((tm, tn), F32)] if nk > 1 and out_dtype != F32 else []
    outs = pl.pallas_call(
        functools.partial(_mm_body, nk=nk, act=act, has_res=res is not None, rot_scale=rot_scale,
                          side_gains=tuple(side_gains), w_t=w_t, emit_norm=emit_norm,
                          norm_width=kdim if row_ss is not None else 0),
        grid=(m // tm, nj, nk),
        in_specs=in_specs,
        out_specs=out_specs,
        out_shape=out_shape,
        scratch_shapes=scratch,
        compiler_params=_params("parallel", "parallel", "arbitrary"),
        name=name,
    )(*args)
    return outs[0] if len(outs) == 1 else tuple(outs)


def _rotary_tables(seq):
    inv_freq = ROPE_THETA ** (-jnp.arange(0, ROPE_DIM, 2, dtype=F32) / ROPE_DIM)
    ang = jnp.arange(seq).astype(F32)[:, None] * inv_freq[None, :]
    cos, sin = jnp.cos(ang), jnp.sin(ang)
    pad = MOBA_HEAD_DIM - ROPE_DIM
    cos_t = jnp.concatenate([cos, cos, jnp.ones((seq, pad), F32)], axis=-1)
    sin_t = jnp.concatenate([-sin, sin, jnp.zeros((seq, pad), F32)], axis=-1)
    return cos_t, sin_t


def _gate_body(a_ref, wl_ref, wu_ref, b_ref, o_ref):
    g_low = lax.dot_general(a_ref[...], wl_ref[...], (((1,), (1,)), ((), ())),
                            preferred_element_type=F32)
    g_hi, g_lo = _split_bf16(g_low, 2)
    w_hi, w_lo = _split_bf16(wu_ref[...], 2)
    z = (jnp.dot(g_hi, w_hi, preferred_element_type=F32) + jnp.dot(g_lo, w_hi, preferred_element_type=F32)
         + jnp.dot(g_hi, w_lo, preferred_element_type=F32))
    z = z + b_ref[...]
    log_sig = jnp.minimum(z, 0.0) - jnp.log1p(jnp.exp(-jnp.abs(z)))
    o_ref[...] = log_sig * (1.0 / GLA_GATE_NORM)


def _gate(xn, w_low, w_up_pad, b, rows=1024):
    m, d = xn.shape
    r = w_low.shape[0]
    n = w_up_pad.shape[1]
    rows = min(rows, m)
    return pl.pallas_call(
        _gate_body,
        grid=(m // rows,),
        in_specs=[pl.BlockSpec((rows, d), lambda i: (i, 0)),
                  pl.BlockSpec((r, d), lambda i: (0, 0)),
                  pl.BlockSpec((r, n), lambda i: (0, 0)),
                  pl.BlockSpec((1, n), lambda i: (0, 0))],
        out_specs=pl.BlockSpec((rows, n), lambda i: (i, 0)),
        out_shape=jax.ShapeDtypeStruct((m, n), F32),
        compiler_params=_params("parallel"),
        name="gla_gate",
    )(xn, w_low, w_up_pad, b.reshape(1, n).astype(F32))


def _gla_body(q_ref, k_ref, v_ref, g_ref, go_ref, gn_ref, o_ref, st_ref, *, chunks):
    c_len = GLA_CHUNK

    @pl.when(pl.program_id(2) == 0)
    def _():
        st_ref[...] = jnp.zeros_like(st_ref)

    row = lax.broadcasted_iota(jnp.int32, (c_len, c_len), 0)
    col = lax.broadcasted_iota(jnp.int32, (c_len, c_len), 1)
    causal = col <= row
    tri = causal.astype(BF16)
    q_scale = GLA_HEAD_K ** -0.5
    gn = gn_ref[...]

    nt = (((1,), (1,)), ((), ()))
    tn = (((0,), (0,)), ((), ()))

    dk = g_ref.shape[1]
    g_all = jnp.concatenate([g_ref[pl.ds(c * c_len, c_len), :] for c in range(chunks)], axis=1)
    b_all = sum(jnp.dot(tri, part, preferred_element_type=F32) for part in _split_bf16(g_all, 3))
    b_last_all = b_all[c_len - 1:c_len, :]
    e_pos = jnp.exp(b_all)
    e_neg = jnp.exp(-b_all)
    e_end = jnp.exp(b_last_all - b_all)
    decay_all = jnp.exp(b_last_all)

    q_decs, o_intras, d_states, decays = [], [], [], []
    for c in range(chunks):
        sl = pl.ds(c * c_len, c_len)
        lanes = slice(c * dk, (c + 1) * dk)
        q = q_ref[sl, :].astype(F32) * q_scale
        k = k_ref[sl, :].astype(F32)
        v = v_ref[sl, :]
        q_dec = (q * e_pos[:, lanes]).astype(BF16)
        k_inv = (k * e_neg[:, lanes]).astype(BF16)
        k_end = (k * e_end[:, lanes]).astype(BF16)
        a = lax.dot_general(q_dec, k_inv, nt, preferred_element_type=F32)
        a = jnp.where(causal, a, 0.0).astype(BF16)
        q_decs.append(q_dec)
        o_intras.append(jnp.dot(a, v, preferred_element_type=F32))
        d_states.append(lax.dot_general(v, k_end, tn, preferred_element_type=F32))
        decays.append(decay_all[:, lanes])

    st = st_ref[...]
    states = []
    for c in range(chunks):
        states.append(st.astype(BF16))
        st = st * decays[c] + d_states[c]
    st_ref[...] = st

    for c in range(chunks):
        sl = pl.ds(c * c_len, c_len)
        o = o_intras[c] + lax.dot_general(q_decs[c], states[c], nt, preferred_element_type=F32)
        ms = jnp.mean(o * o, axis=-1, keepdims=True)
        y = o * lax.rsqrt(ms + EPS) * gn
        go = go_ref[sl, :].astype(F32)
        y = y * (go * jax.nn.sigmoid(go))
        o_ref[sl, :] = y.astype(o_ref.dtype)


def _gla(qkv, g, g_out, gn, *, batch, seq, heads, rows=1024):
    m = batch * seq
    rows = min(rows, seq)
    steps = seq // rows
    hk, hv = GLA_HEAD_K, GLA_HEAD_V
    k_off = heads
    v_off = heads * 2 * hk // hv
    row_map = lambda b, h, t: b * steps + t
    return pl.pallas_call(
        functools.partial(_gla_body, chunks=rows // GLA_CHUNK),
        grid=(batch, heads, steps),
        in_specs=[pl.BlockSpec((rows, hk), lambda b, h, t: (row_map(b, h, t), h)),
                  pl.BlockSpec((rows, hk), lambda b, h, t: (row_map(b, h, t), k_off + h)),
                  pl.BlockSpec((rows, hv), lambda b, h, t: (row_map(b, h, t), v_off + h)),
                  pl.BlockSpec((rows, hk), lambda b, h, t: (row_map(b, h, t), h)),
                  pl.BlockSpec((rows, hv), lambda b, h, t: (row_map(b, h, t), h)),
                  pl.BlockSpec((1, hv), lambda b, h, t: (0, 0))],
        out_specs=pl.BlockSpec((rows, hv), lambda b, h, t: (row_map(b, h, t), h)),
        out_shape=jax.ShapeDtypeStruct((m, heads * hv), BF16),
        scratch_shapes=[pltpu.VMEM((hv, hk), F32)],
        compiler_params=_params("parallel", "parallel", "arbitrary"),
        name="gla",
    )(qkv, qkv, qkv, g, g_out, gn.reshape(1, hv).astype(F32))


def _moba_body(q_ref, qn_ref, k_ref, v_ref, o_ref, kaug_ref, vaug_ref, kmh_ref, kml_ref, bias_ref,
               *, n_blocks, group, hp):
    blk = MOBA_BLOCK
    dh = MOBA_HEAD_DIM
    seq = n_blocks * blk
    km_rows = kmh_ref.shape[1]
    qb = pl.program_id(2)
    nt = (((1,), (1,)), ((), ()))

    @pl.when(qb == 0)
    def _():
        r = lax.broadcasted_iota(jnp.int32, (seq, dh), 0)
        l = lax.broadcasted_iota(jnp.int32, (seq, dh), 1)
        one_hot = ((l * blk <= r) & (r < (l + 1) * blk)).astype(BF16)
        ones = jnp.ones((seq, dh), BF16)
        j = lax.broadcasted_iota(jnp.int32, (km_rows, seq), 0)
        s = lax.broadcasted_iota(jnp.int32, (km_rows, seq), 1)
        avg = jnp.where((j * blk <= s) & (s < (j + 1) * blk), 1.0 / blk, 0.0).astype(BF16)
        for h in range(hp):
            k = k_ref[:, h * dh:(h + 1) * dh]
            kaug_ref[h, :, :dh] = k
            kaug_ref[h, :, dh:] = one_hot
            vaug_ref[h, :, :dh] = v_ref[:, h * dh:(h + 1) * dh]
            vaug_ref[h, :, dh:] = ones
            km = jnp.dot(avg, k, preferred_element_type=F32)
            hi = km.astype(BF16)
            kmh_ref[h] = hi
            kml_ref[h] = (km - hi.astype(F32)).astype(BF16)

    row = lax.broadcasted_iota(jnp.int32, (km_rows, blk), 0)
    qi = lax.broadcasted_iota(jnp.int32, (blk, blk), 0)
    ki = lax.broadcasted_iota(jnp.int32, (blk, blk), 1)
    causal = ki <= qi
    start = pl.multiple_of(qb * blk, blk)
    slot = qb % 2

    def select_next(h):
        qn = qn_ref[:, h * dh:(h + 1) * dh]
        gate = (lax.dot_general(kmh_ref[h], qn, nt, preferred_element_type=F32)
                + lax.dot_general(kml_ref[h], qn, nt, preferred_element_type=F32))
        past = row < qb + 1
        gm = jnp.where(past, gate, -jnp.inf)
        rank = jnp.zeros((km_rows, blk), jnp.int32)
        for j in range(n_blocks - 1):
            gj = gm[j:j + 1, :]
            beats = (gj > gm) | ((gj == gm) & (row > j))
            rank = rank + beats.astype(jnp.int32)
        sel = past & (rank < MOBA_TOPK)
        bias_t = jnp.where(sel, 0.0, MASK_VALUE)
        bias_t = jnp.concatenate([bias_t, jnp.zeros((dh - km_rows, blk), F32)], axis=0)
        bias_ref[1 - slot, h] = bias_t.T.astype(BF16)

    def attend(nb):
        for h in range(hp):
            cols = slice(h * dh, (h + 1) * dh)
            q = q_ref[:, cols]
            s_own = lax.dot_general(q, kaug_ref[h, pl.ds(start, blk), :dh], nt, preferred_element_type=F32)
            s_own = jnp.where(causal, s_own, MASK_VALUE)
            m = jnp.max(s_own, axis=-1, keepdims=True)
            v_own = vaug_ref[h, pl.ds(start, blk), :]
            if nb == 0:
                p_own = jnp.exp2((s_own - m).astype(BF16))
                acc = jnp.dot(p_own, v_own, preferred_element_type=F32)
            else:
                q_aug = jnp.concatenate([q, bias_ref[slot, h]], axis=1)
                s = lax.dot_general(q_aug, kaug_ref[h, :nb * blk, :], nt, preferred_element_type=F32)
                m = jnp.maximum(m, jnp.max(s, axis=-1, keepdims=True))
                p_own = jnp.exp2((s_own - m).astype(BF16))
                p = jnp.exp2((s - m).astype(BF16))
                acc = (jnp.dot(p_own, v_own, preferred_element_type=F32)
                       + jnp.dot(p, vaug_ref[h, :nb * blk, :], preferred_element_type=F32))
            o_ref[:, cols] = (acc[:, :dh] / acc[:, dh:dh + 1]).astype(o_ref.dtype)
        for h in range(hp):
            select_next(h)

    n_cls = -(-(n_blocks - 1) // group)
    cls = (qb + (group - 1)) // group
    for c in range(n_cls + 1):
        pl.when(cls == c)(functools.partial(attend, min(c * group, n_blocks)))


def _moba(q, k, v, *, batch, seq, heads, hp=MOBA_HEADS_PER_STEP):
    m = batch * seq
    blk, dh = MOBA_BLOCK, MOBA_HEAD_DIM
    n_blocks = seq // blk
    km_rows = -(-n_blocks // BF16_SUBLANES) * BF16_SUBLANES
    return pl.pallas_call(
        functools.partial(_moba_body, n_blocks=n_blocks, group=MOBA_KEY_GROUP, hp=hp),
        grid=(batch, heads // hp, n_blocks),
        in_specs=[pl.BlockSpec((blk, hp * dh), lambda b, h, t: (b * n_blocks + t, h)),
                  pl.BlockSpec((blk, hp * dh),
                               lambda b, h, t: (b * n_blocks + jnp.minimum(t + 1, n_blocks - 1), h)),
                  pl.BlockSpec((seq, hp * dh), lambda b, h, t: (b, h)),
                  pl.BlockSpec((seq, hp * dh), lambda b, h, t: (b, h))],
        out_specs=pl.BlockSpec((blk, hp * dh), lambda b, h, t: (b * n_blocks + t, h)),
        out_shape=jax.ShapeDtypeStruct((m, heads * dh), BF16),
        scratch_shapes=[pltpu.VMEM((hp, seq, 2 * dh), BF16),
                        pltpu.VMEM((hp, seq, 2 * dh), BF16),
                        pltpu.VMEM((hp, km_rows, dh), BF16),
                        pltpu.VMEM((hp, km_rows, dh), BF16),
                        pltpu.VMEM((2, hp, blk, dh), BF16)],
        compiler_params=_params("parallel", "parallel", "arbitrary"),
        name="moba",
    )(q, q, k, v)


def _mix_out_body(a1_ref, a2_ref, w_ref, r_ref, o_ref, hb_ref, ss_ref):
    half = a1_ref.shape[1]
    acc = jnp.dot(a1_ref[...], w_ref[:half, :], preferred_element_type=F32)
    acc = acc + jnp.dot(a2_ref[...], w_ref[half:, :], preferred_element_type=F32)
    h = acc + r_ref[...]
    o_ref[...] = h
    _emit_norm_parts(h, hb_ref, ss_ref)


def _mix_out(a1, a2, w, res, tm=1024, tn=1024):
    m, half = a1.shape
    n = w.shape[1]
    tm, tn = min(tm, m), min(tn, n)
    return pl.pallas_call(
        _mix_out_body,
        grid=(m // tm, n // tn),
        in_specs=[pl.BlockSpec((tm, half), lambda i, j: (i, 0)),
                  pl.BlockSpec((tm, half), lambda i, j: (i, 0)),
                  pl.BlockSpec((2 * half, tn), lambda i, j: (0, j)),
                  pl.BlockSpec((tm, tn), lambda i, j: (i, j))],
        out_specs=[pl.BlockSpec((tm, tn), lambda i, j: (i, j)),
                   pl.BlockSpec((tm, tn), lambda i, j: (i, j)),
                   pl.BlockSpec((tm, LANES), lambda i, j: (i, j))],
        out_shape=[jax.ShapeDtypeStruct((m, n), F32), jax.ShapeDtypeStruct((m, n), BF16),
                   jax.ShapeDtypeStruct((m, (n // tn) * LANES), F32)],
        compiler_params=_params("parallel", "parallel"),
        name="mix_out",
    )(a1, a2, w, res)


def _cross_body(q_ref, k_ref, v_ref, o_ref, *, heads):
    d = q_ref.shape[1]
    dh = d // heads
    scale = dh ** -0.5
    for h in range(heads):
        sl = slice(h * dh, (h + 1) * dh)
        s = lax.dot_general(q_ref[:, sl], k_ref[:, sl], (((1,), (1,)), ((), ())),
                            preferred_element_type=F32) * scale
        s = s - jnp.max(s, axis=-1, keepdims=True)
        e = jnp.exp(s)
        p = e / jnp.sum(e, axis=-1, keepdims=True)
        o_ref[:, sl] = jnp.dot(p.astype(BF16), v_ref[:, sl],
                               preferred_element_type=F32).astype(o_ref.dtype)


def _cross(q, k, v, *, batch, seq, n_mem, heads, rows=512):
    m, d = q.shape
    rows = min(rows, seq)
    steps = seq // rows
    return pl.pallas_call(
        functools.partial(_cross_body, heads=heads),
        grid=(batch, steps),
        in_specs=[pl.BlockSpec((rows, d), lambda b, t: (b * steps + t, 0)),
                  pl.BlockSpec((n_mem, d), lambda b, t: (b, 0)),
                  pl.BlockSpec((n_mem, d), lambda b, t: (b, 0))],
        out_specs=pl.BlockSpec((rows, d), lambda b, t: (b * steps + t, 0)),
        out_shape=jax.ShapeDtypeStruct((m, d), BF16),
        compiler_params=_params("parallel", "parallel"),
        name="cross_attn",
    )(q, k, v)


def _layer(h, mem2, batch, seq, n_mem, norm_mix_g, w_in, w_gate_up, b_gate, gla_norm_g, w_out,
           norm_cross_g, norm_mem_g, w_cq, w_ck, w_cv, w_co, norm_mlp_g, w_up, w_down):
    m, d = h.shape
    gla_dk = GLA_HEADS * GLA_HEAD_K
    gla_dv = GLA_HEADS * GLA_HEAD_V
    moba_w = d - gla_dv
    moba_heads = moba_w // MOBA_HEAD_DIM
    o_glow = 2 * gla_dk + gla_dv
    o_gout = o_glow + GLA_GATE_RANK
    in_width = w_in.shape[1]
    w_gu = jnp.pad(w_gate_up, ((0, LANES - GLA_GATE_RANK), (0, 0)))
    w_in_t = jnp.swapaxes(w_in, 0, 1)

    xn, w_gla_t = _rmsnorm(h, norm_mix_g, BF16, side=(w_in_t, o_glow))
    gla_qkv, w_rest_t, w_glow_t, w_out_b = _matmul(
        xn, w_gla_t, w_t=True, out_dtype=BF16, name="proj_gla_qkv",
        sides=[_side_rows(w_in_t, o_gout, in_width - o_gout), _side_fixed(w_in_t, o_glow, GLA_GATE_RANK),
               _side_all(w_out)])
    c_mq = gla_dv
    c_mk = c_mq + moba_w
    c_mv = c_mk + moba_w
    g = _gate(xn, jnp.pad(w_glow_t, ((0, LANES - GLA_GATE_RANK), (0, 0))), w_gu, b_gate)
    g_out, w_cq_b = _matmul(xn, w_rest_t, w_t=True, col0=0, n=gla_dv, out_dtype=BF16, name="proj_gla_gout",
                            sides=[_side_all(w_cq, norm_cross_g)])
    o_gla = _gla(gla_qkv, g, g_out, gla_norm_g, batch=batch, seq=seq, heads=GLA_HEADS)

    cos_t, sin_t = _rotary_tables(seq)
    mq, w_ck_b = _matmul(xn, w_rest_t, w_t=True, col0=c_mq, n=moba_w, out_dtype=BF16, name="proj_moba_q",
                         rot=(cos_t, sin_t, MOBA_HEAD_DIM ** -0.5 * LOG2E), sides=[_side_all(w_ck)])
    mk, w_cv_b = _matmul(xn, w_rest_t, w_t=True, col0=c_mk, n=moba_w, out_dtype=BF16, name="proj_moba_k",
                         rot=(cos_t, sin_t, 1.0), sides=[_side_all(w_cv)])
    mv, w_co_b = _matmul(xn, w_rest_t, w_t=True, col0=c_mv, n=moba_w, out_dtype=BF16, name="proj_moba_v",
                         sides=[_side_all(w_co)])
    o_moba = _moba(mq, mk, mv, batch=batch, seq=seq, heads=moba_heads)

    h, h_b, h_ss = _mix_out(o_gla, o_moba, w_out_b, h)
    mem_n = _rmsnorm(mem2, norm_mem_g, BF16)
    cq, w_up_b = _matmul(h_b, w_cq_b, row_ss=h_ss, out_dtype=BF16, name="proj_cross_q",
                         sides=[_side_all(w_up, norm_mlp_g)])
    ck = _matmul(mem_n, w_ck_b, out_dtype=BF16, name="proj_cross_k")
    cv = _matmul(mem_n, w_cv_b, out_dtype=BF16, name="proj_cross_v")
    o_cross = _cross(cq, ck, cv, batch=batch, seq=seq, n_mem=n_mem, heads=CROSS_HEADS)
    h, h_b, h_ss = _matmul(o_cross, w_co_b, out_dtype=F32, res=h, emit_norm=True, name="proj_cross_out")

    u, w_down_b = _matmul(h_b, w_up_b, row_ss=h_ss, out_dtype=BF16, act="relu2", name="mlp_up",
                          sides=[_side_all(w_down)])
    h = _matmul(u, w_down_b, out_dtype=F32, res=h, tk=4096, name="mlp_down")
    return h


def kernel(x, mem, norm_mix_g, w_in, w_gate_up, b_gate, gla_norm_g, w_out, norm_cross_g, norm_mem_g,
           w_cq, w_ck, w_cv, w_co, norm_mlp_g, w_up, w_down, norm_final_g):
    batch, seq, d = x.shape
    n_mem = mem.shape[1]
    h = x.reshape(batch * seq, d)
    mem2 = mem.reshape(batch * n_mem, d)
    for l in range(norm_mix_g.shape[0]):
        h = _layer(h, mem2, batch, seq, n_mem, norm_mix_g[l], w_in[l], w_gate_up[l], b_gate[l],
                   gla_norm_g[l], w_out[l], norm_cross_g[l], norm_mem_g[l], w_cq[l], w_ck[l],
                   w_cv[l], w_co[l], norm_mlp_g[l], w_up[l], w_down[l])
    out = _rmsnorm(h, norm_final_g, x.dtype)
    return out.reshape(batch, seq, d)
```

```python
import functools

import jax
import jax.numpy as jnp
from jax import lax
from jax.experimental import pallas as pl
from jax.experimental.pallas import tpu as pltpu

F32 = jnp.float32
BF16 = jnp.bfloat16

EPS = 1e-6
LANES = 128
BF16_SUBLANES = 16
VMEM_LIMIT_BYTES = 62 * 1024 * 1024

GLA_HEADS = 8
GLA_HEAD_K = 128
GLA_HEAD_V = 256
GLA_GATE_RANK = 16
GLA_GATE_NORM = 16.0
GLA_CHUNK = 64
MOBA_HEAD_DIM = 128
MOBA_BLOCK = 256
MOBA_TOPK = 3
MOBA_KEY_GROUP = 2
MOBA_HEADS_PER_STEP = 4
ROPE_THETA = 500000.0
ROPE_DIM = MOBA_HEAD_DIM // 4
ROPE_HALF = ROPE_DIM // 2
CROSS_HEADS = 4
MASK_VALUE = -1e30
LOG2E = 1.4426950408889634


def _split_bf16(x, terms):
    parts = []
    for _ in range(terms - 1):
        p = x.astype(BF16)
        parts.append(p)
        x = x - p.astype(F32)
    parts.append(x.astype(BF16))
    return parts


def _params(*sem):
    return pltpu.CompilerParams(dimension_semantics=sem, vmem_limit_bytes=VMEM_LIMIT_BYTES)


def _rmsnorm_body(x_ref, g_ref, o_ref):
    x = x_ref[...].astype(F32)
    ms = jnp.mean(x * x, axis=-1, keepdims=True)
    y = x * lax.rsqrt(ms + EPS)
    o_ref[...] = (y * g_ref[...]).astype(o_ref.dtype)


def _rmsnorm(x, g, out_dtype, rows=256):
    m, d = x.shape
    rows = min(rows, m)
    return pl.pallas_call(
        _rmsnorm_body,
        grid=(m // rows,),
        in_specs=[pl.BlockSpec((rows, d), lambda i: (i, 0)),
                  pl.BlockSpec((1, d), lambda i: (0, 0))],
        out_specs=pl.BlockSpec((rows, d), lambda i: (i, 0)),
        out_shape=jax.ShapeDtypeStruct((m, d), out_dtype),
        compiler_params=_params("parallel"),
        name="rmsnorm",
    )(x, g.reshape(1, d).astype(F32))


def _norm_gate_body(x_ref, g_ref, src_ref, wl_ref, wu_ref, b_ref, o_ref, dst_ref, gate_ref):
    x = x_ref[...].astype(F32)
    ms = jnp.mean(x * x, axis=-1, keepdims=True)
    xn = (x * lax.rsqrt(ms + EPS) * g_ref[...]).astype(BF16)
    o_ref[...] = xn
    dst_ref[...] = src_ref[...].astype(BF16)
    g_low = lax.dot_general(xn, wl_ref[...].astype(BF16), (((1,), (1,)), ((), ())),
                            preferred_element_type=F32)
    g_hi, g_lo = _split_bf16(g_low, 2)
    w_hi, w_lo = _split_bf16(wu_ref[...], 2)
    z = (jnp.dot(g_hi, w_hi, preferred_element_type=F32) + jnp.dot(g_lo, w_hi, preferred_element_type=F32)
         + jnp.dot(g_hi, w_lo, preferred_element_type=F32))
    z = z + b_ref[...]
    log_sig = jnp.minimum(z, 0.0) - jnp.log1p(jnp.exp(-jnp.abs(z)))
    gate_ref[...] = log_sig * (1.0 / GLA_GATE_NORM)


def _norm_gate(x, g, side, w_low_t, w_up_pad, b, rows=256):
    m, d = x.shape
    rows = min(rows, m)
    steps = m // rows
    src, n_rows = side
    cols = src.shape[1]
    assert n_rows % steps == 0 and (n_rows // steps) % BF16_SUBLANES == 0 and src.dtype == F32
    r = n_rows // steps
    rank_pad, n_gate = w_up_pad.shape
    assert w_low_t.shape == (rank_pad, d)
    return pl.pallas_call(
        _norm_gate_body,
        grid=(steps,),
        in_specs=[pl.BlockSpec((rows, d), lambda i: (i, 0)),
                  pl.BlockSpec((1, d), lambda i: (0, 0)),
                  pl.BlockSpec((r, cols), lambda i: (i, 0)),
                  pl.BlockSpec((rank_pad, d), lambda i: (0, 0)),
                  pl.BlockSpec((rank_pad, n_gate), lambda i: (0, 0)),
                  pl.BlockSpec((1, n_gate), lambda i: (0, 0))],
        out_specs=[pl.BlockSpec((rows, d), lambda i: (i, 0)),
                   pl.BlockSpec((r, cols), lambda i: (i, 0)),
                   pl.BlockSpec((rows, n_gate), lambda i: (i, 0))],
        out_shape=[jax.ShapeDtypeStruct((m, d), BF16),
                   jax.ShapeDtypeStruct((n_rows, cols), BF16),
                   jax.ShapeDtypeStruct((m, n_gate), F32)],
        compiler_params=_params("parallel"),
        name="norm_gate",
    )(x, g.reshape(1, d).astype(F32), src, w_low_t, w_up_pad, b.reshape(1, n_gate).astype(F32))


def _emit_norm_parts(h, hb_ref, ss_ref):
    hb_ref[...] = h.astype(BF16)
    ss_ref[...] = jnp.broadcast_to(jnp.sum(h * h, axis=1, keepdims=True), ss_ref.shape)


def _mm_body(*refs, nk, act, has_res, rot_scale, side_gains, w_t, emit_norm, norm_width):
    it = iter(refs)
    a_ref, w_ref = next(it), next(it)
    r_ref = next(it) if has_res else None
    cos_ref, sin_ref = (next(it), next(it)) if rot_scale is not None else (None, None)
    ss_in_ref = next(it) if norm_width else None
    side_in = [(next(it), next(it) if gain else None) for gain in side_gains]
    o_ref = next(it)
    hb_ref, ss_ref = (next(it), next(it)) if emit_norm else (None, None)
    side_out = [next(it) for _ in side_gains]
    acc_ref = next(it) if nk > 1 and o_ref.dtype != F32 else o_ref

    for (src_ref, gain_ref), dst_ref in zip(side_in, side_out):
        src = src_ref[...]
        if gain_ref is not None:
            src = src * gain_ref[...]
        dst_ref[...] = src.astype(BF16)

    def product(cols=None):
        w = w_ref[...] if cols is None else (w_ref[cols, :] if w_t else w_ref[:, cols])
        if w_t:
            return lax.dot_general(a_ref[...], w, (((1,), (1,)), ((), ())), preferred_element_type=F32)
        return jnp.dot(a_ref[...], w, preferred_element_type=F32)

    def finish(acc):
        if norm_width:
            ms = jnp.sum(ss_in_ref[...], axis=1, keepdims=True) * (1.0 / (LANES * norm_width))
            acc = acc * lax.rsqrt(ms + EPS)
        if act == "relu2":
            acc = jnp.square(jnp.maximum(acc, 0.0))
        if has_res:
            acc = acc + r_ref[...]
        o_ref[...] = acc.astype(o_ref.dtype)
        if emit_norm:
            _emit_norm_parts(acc, hb_ref, ss_ref)

    def finish_rotary():
        tm, tn = o_ref.shape
        cos = cos_ref[...]
        sin = sin_ref[...]
        lane = lax.broadcasted_iota(jnp.int32, (tm, LANES), 1)
        half = tn // 2
        for c0 in (0, half):
            acc = product(slice(c0, c0 + half))
            for c in range(half // LANES):
                xs = acc[:, c * LANES:(c + 1) * LANES]
                partner = jnp.where(lane < ROPE_HALF,
                                    pltpu.roll(xs, LANES - ROPE_HALF, 1),
                                    pltpu.roll(xs, ROPE_HALF, 1))
                y = xs * cos + partner * sin
                if rot_scale != 1.0:
                    y = y * rot_scale
                o_ref[:, c0 + c * LANES:c0 + (c + 1) * LANES] = y.astype(o_ref.dtype)

    if rot_scale is not None:
        finish_rotary()
        return
    if nk == 1:
        finish(product())
        return

    k = pl.program_id(2)

    @pl.when(k == 0)
    def _():
        acc_ref[...] = product()

    @pl.when(k > 0)
    def _():
        acc_ref[...] += product()

    @pl.when(k == nk - 1)
    def _():
        finish(acc_ref[...])


def _side_all(src, gain=None):
    return (src, 0, src.shape[0], gain)


def _side_rows(src, r0, nr):
    return (src, r0, nr, None)


def _matmul(a, w, *, out_dtype, res=None, act=None, rot=None, w_t=False, col0=0, n=None, sides=(),
            row_ss=None, emit_norm=False, tm=1024, tn=1024, tk=4096, name="matmul"):
    m, kdim = a.shape
    n_total = w.shape[0] if w_t else w.shape[1]
    n = n_total if n is None else n
    tm, tn, tk = min(tm, m), min(tn, n), min(tk, kdim)
    if rot is not None:
        tm = min(tm, rot[0].shape[0])
    nk = kdim // tk
    assert col0 % tn == 0 and n % tn == 0 and m % tm == 0 and kdim % tk == 0
    j0 = col0 // tn
    nj = n // tn
    steps = (m // tm) * nj * nk
    w_spec = (pl.BlockSpec((tn, tk), lambda i, j, k: (j0 + j, k)) if w_t
              else pl.BlockSpec((tk, tn), lambda i, j, k: (k, j0 + j)))
    in_specs = [pl.BlockSpec((tm, tk), lambda i, j, k: (i, k)), w_spec]
    args = [a, w]
    if res is not None:
        in_specs.append(pl.BlockSpec((tm, tn), lambda i, j, k: (i, j)))
        args.append(res)
    if rot is not None:
        cos_t, sin_t, rot_scale = rot
        assert nk == 1 and cos_t.shape[0] % tm == 0 and (tn // 2) % LANES == 0
        tiles_per_seq = cos_t.shape[0] // tm
        in_specs += [pl.BlockSpec((tm, LANES), lambda i, j, k: (i % tiles_per_seq, 0))] * 2
        args += [cos_t, sin_t]
    else:
        rot_scale = None
    if row_ss is not None:
        assert nk == 1 and row_ss.shape[0] == m
        in_specs.append(pl.BlockSpec((tm, row_ss.shape[1]), lambda i, j, k: (i, 0)))
        args.append(row_ss)
    step = lambda i, j, k: (i * nj + j) * nk + k
    out_specs = [pl.BlockSpec((tm, tn), lambda i, j, k: (i, j))]
    out_shape = [jax.ShapeDtypeStruct((m, n), out_dtype)]
    if emit_norm:
        assert out_dtype == F32
        out_specs += [pl.BlockSpec((tm, tn), lambda i, j, k: (i, j)),
                      pl.BlockSpec((tm, LANES), lambda i, j, k: (i, j))]
        out_shape += [jax.ShapeDtypeStruct((m, n), BF16), jax.ShapeDtypeStruct((m, nj * LANES), F32)]
    side_gains = []
    for src, r0, nr, gain in sides:
        assert src.dtype == F32 and src.ndim == 2
        cols = src.shape[1]
        assert r0 % BF16_SUBLANES == 0 and nr % steps == 0 and (nr // steps) % BF16_SUBLANES == 0
        r = nr // steps
        if r0 % r == 0:
            in_specs.append(pl.BlockSpec((r, cols), lambda i, j, k, b=r0 // r: (b + step(i, j, k), 0)))
        else:
            in_specs.append(pl.BlockSpec((pl.Element(r), pl.Element(cols)),
                                         lambda i, j, k, r0=r0, r=r: (
                                             pl.multiple_of(r0 + step(i, j, k) * r, BF16_SUBLANES), 0)))
        args.append(src)
        if gain is not None:
            assert r0 == 0 and gain.shape == (nr,)
            in_specs.append(pl.BlockSpec((r, 1), lambda i, j, k: (step(i, j, k), 0)))
            args.append(gain.reshape(nr, 1).astype(F32))
        out_specs.append(pl.BlockSpec((r, cols), lambda i, j, k: (step(i, j, k), 0)))
        out_shape.append(jax.ShapeDtypeStruct((nr, cols), BF16))
        side_gains.append(gain is not None)
    scratch = [pltpu.VMEM((tm, tn), F32)] if nk > 1 and out_dtype != F32 else []
    outs = pl.pallas_call(
        functools.partial(_mm_body, nk=nk, act=act, has_res=res is not None, rot_scale=rot_scale,
                          side_gains=tuple(side_gains), w_t=w_t, emit_norm=emit_norm,
                          norm_width=kdim if row_ss is not None else 0),
        grid=(m // tm, nj, nk),
        in_specs=in_specs,
        out_specs=out_specs,
        out_shape=out_shape,
        scratch_shapes=scratch,
        compiler_params=_params("parallel", "parallel", "arbitrary"),
        name=name,
    )(*args)
    return outs[0] if len(outs) == 1 else tuple(outs)


def _rotary_tables(seq):
    inv_freq = ROPE_THETA ** (-jnp.arange(0, ROPE_DIM, 2, dtype=F32) / ROPE_DIM)
    ang = jnp.arange(seq).astype(F32)[:, None] * inv_freq[None, :]
    cos, sin = jnp.cos(ang), jnp.sin(ang)
    pad = MOBA_HEAD_DIM - ROPE_DIM
    cos_t = jnp.concatenate([cos, cos, jnp.ones((seq, pad), F32)], axis=-1)
    sin_t = jnp.concatenate([-sin, sin, jnp.zeros((seq, pad), F32)], axis=-1)
    return cos_t, sin_t


def _gla_body(q_ref, k_ref, v_ref, g_ref, go_ref, gn_ref, o_ref, st_ref, *, chunks):
    c_len = GLA_CHUNK

    @pl.when(pl.program_id(2) == 0)
    def _():
        st_ref[...] = jnp.zeros_like(st_ref)

    row = lax.broadcasted_iota(jnp.int32, (c_len, c_len), 0)
    col = lax.broadcasted_iota(jnp.int32, (c_len, c_len), 1)
    causal = col <= row
    tri = causal.astype(BF16)
    q_scale = GLA_HEAD_K ** -0.5
    gn = gn_ref[...]

    nt = (((1,), (1,)), ((), ()))
    tn = (((0,), (0,)), ((), ()))

    dk = g_ref.shape[1]
    g_all = jnp.concatenate([g_ref[pl.ds(c * c_len, c_len), :] for c in range(chunks)], axis=1)
    b_all = sum(jnp.dot(tri, part, preferred_element_type=F32) for part in _split_bf16(g_all, 3))
    b_last_all = b_all[c_len - 1:c_len, :]
    e_pos = jnp.exp(b_all)
    e_neg = jnp.exp(-b_all)
    e_end = jnp.exp(b_last_all - b_all)
    decay_all = jnp.exp(b_last_all)

    q_decs, o_intras, d_states, decays = [], [], [], []
    for c in range(chunks):
        sl = pl.ds(c * c_len, c_len)
        lanes = slice(c * dk, (c + 1) * dk)
        q = q_ref[sl, :].astype(F32) * q_scale
        k = k_ref[sl, :].astype(F32)
        v = v_ref[sl, :]
        q_dec = (q * e_pos[:, lanes]).astype(BF16)
        k_inv = (k * e_neg[:, lanes]).astype(BF16)
        k_end = (k * e_end[:, lanes]).astype(BF16)
        a = lax.dot_general(q_dec, k_inv, nt, preferred_element_type=F32)
        a = jnp.where(causal, a, 0.0).astype(BF16)
        q_decs.append(q_dec)
        o_intras.append(jnp.dot(a, v, preferred_element_type=F32))
        d_states.append(lax.dot_general(v, k_end, tn, preferred_element_type=F32))
        decays.append(decay_all[:, lanes])

    st = st_ref[...]
    states = []
    for c in range(chunks):
        states.append(st.astype(BF16))
        st = st * decays[c] + d_states[c]
    st_ref[...] = st

    for c in range(chunks):
        sl = pl.ds(c * c_len, c_len)
        o = o_intras[c] + lax.dot_general(q_decs[c], states[c], nt, preferred_element_type=F32)
        ms = jnp.mean(o * o, axis=-1, keepdims=True)
        y = o * lax.rsqrt(ms + EPS) * gn
        go = go_ref[sl, :].astype(F32)
        y = y * (go * jax.nn.sigmoid(go))
        o_ref[sl, :] = y.astype(o_ref.dtype)


def _gla(qkv, g, g_out, gn, *, batch, seq, heads, rows=1024):
    m = batch * seq
    rows = min(rows, seq)
    steps = seq // rows
    hk, hv = GLA_HEAD_K, GLA_HEAD_V
    k_off = heads
    v_off = heads * 2 * hk // hv
    row_map = lambda b, h, t: b * steps + t
    return pl.pallas_call(
        functools.partial(_gla_body, chunks=rows // GLA_CHUNK),
        grid=(batch, heads, steps),
        in_specs=[pl.BlockSpec((rows, hk), lambda b, h, t: (row_map(b, h, t), h)),
                  pl.BlockSpec((rows, hk), lambda b, h, t: (row_map(b, h, t), k_off + h)),
                  pl.BlockSpec((rows, hv), lambda b, h, t: (row_map(b, h, t), v_off + h)),
                  pl.BlockSpec((rows, hk), lambda b, h, t: (row_map(b, h, t), h)),
                  pl.BlockSpec((rows, hv), lambda b, h, t: (row_map(b, h, t), h)),
                  pl.BlockSpec((1, hv), lambda b, h, t: (0, 0))],
        out_specs=pl.BlockSpec((rows, hv), lambda b, h, t: (row_map(b, h, t), h)),
        out_shape=jax.ShapeDtypeStruct((m, heads * hv), BF16),
        scratch_shapes=[pltpu.VMEM((hv, hk), F32)],
        compiler_params=_params("parallel", "parallel", "arbitrary"),
        name="gla",
    )(qkv, qkv, qkv, g, g_out, gn.reshape(1, hv).astype(F32))


def _moba_body(q_ref, qn_ref, k_ref, v_ref, o_ref, kaug_ref, vaug_ref, kmh_ref, kml_ref, bias_ref,
               *, n_blocks, group, hp):
    blk = MOBA_BLOCK
    dh = MOBA_HEAD_DIM
    seq = n_blocks * blk
    km_rows = kmh_ref.shape[1]
    qb = pl.program_id(2)
    nt = (((1,), (1,)), ((), ()))

    @pl.when(qb == 0)
    def _():
        r = lax.broadcasted_iota(jnp.int32, (seq, dh), 0)
        l = lax.broadcasted_iota(jnp.int32, (seq, dh), 1)
        one_hot = ((l * blk <= r) & (r < (l + 1) * blk)).astype(BF16)
        ones = jnp.ones((seq, dh), BF16)
        j = lax.broadcasted_iota(jnp.int32, (km_rows, seq), 0)
        s = lax.broadcasted_iota(jnp.int32, (km_rows, seq), 1)
        avg = jnp.where((j * blk <= s) & (s < (j + 1) * blk), 1.0 / blk, 0.0).astype(BF16)
        for h in range(hp):
            k = k_ref[:, h * dh:(h + 1) * dh]
            kaug_ref[h, :, :dh] = k
            kaug_ref[h, :, dh:] = one_hot
            vaug_ref[h, :, :dh] = v_ref[:, h * dh:(h + 1) * dh]
            vaug_ref[h, :, dh:] = ones
            km = jnp.dot(avg, k, preferred_element_type=F32)
            hi = km.astype(BF16)
            kmh_ref[h] = hi
            kml_ref[h] = (km - hi.astype(F32)).astype(BF16)

    row = lax.broadcasted_iota(jnp.int32, (km_rows, blk), 0)
    qi = lax.broadcasted_iota(jnp.int32, (blk, blk), 0)
    ki = lax.broadcasted_iota(jnp.int32, (blk, blk), 1)
    causal = ki <= qi
    start = pl.multiple_of(qb * blk, blk)
    slot = qb % 2

    def select_next(h):
        qn = qn_ref[:, h * dh:(h + 1) * dh]
        gate = (lax.dot_general(kmh_ref[h], qn, nt, preferred_element_type=F32)
                + lax.dot_general(kml_ref[h], qn, nt, preferred_element_type=F32))
        past = row < qb + 1
        gm = jnp.where(past, gate, -jnp.inf)
        rank = jnp.zeros((km_rows, blk), jnp.int32)
        for j in range(n_blocks - 1):
            gj = gm[j:j + 1, :]
            beats = (gj > gm) | ((gj == gm) & (row > j))
            rank = rank + beats.astype(jnp.int32)
        sel = past & (rank < MOBA_TOPK)
        bias_t = jnp.where(sel, 0.0, MASK_VALUE)
        bias_t = jnp.concatenate([bias_t, jnp.zeros((dh - km_rows, blk), F32)], axis=0)
        bias_ref[1 - slot, h] = bias_t.T.astype(BF16)

    def attend(nb):
        for h in range(hp):
            cols = slice(h * dh, (h + 1) * dh)
            q = q_ref[:, cols]
            s_own = lax.dot_general(q, kaug_ref[h, pl.ds(start, blk), :dh], nt, preferred_element_type=F32)
            s_own = jnp.where(causal, s_own, MASK_VALUE)
            m = jnp.max(s_own, axis=-1, keepdims=True)
            v_own = vaug_ref[h, pl.ds(start, blk), :]
            if nb == 0:
                p_own = jnp.exp2((s_own - m).astype(BF16))
                acc = jnp.dot(p_own, v_own, preferred_element_type=F32)
            else:
                q_aug = jnp.concatenate([q, bias_ref[slot, h]], axis=1)
                s = lax.dot_general(q_aug, kaug_ref[h, :nb * blk, :], nt, preferred_element_type=F32)
                m = jnp.maximum(m, jnp.max(s, axis=-1, keepdims=True))
                p_own = jnp.exp2((s_own - m).astype(BF16))
                p = jnp.exp2((s - m).astype(BF16))
                acc = (jnp.dot(p_own, v_own, preferred_element_type=F32)
                       + jnp.dot(p, vaug_ref[h, :nb * blk, :], preferred_element_type=F32))
            o_ref[:, cols] = (acc[:, :dh] / acc[:, dh:dh + 1]).astype(o_ref.dtype)
        for h in range(hp):
            select_next(h)

    n_cls = -(-(n_blocks - 1) // group)
    cls = (qb + (group - 1)) // group
    for c in range(n_cls + 1):
        pl.when(cls == c)(functools.partial(attend, min(c * group, n_blocks)))


def _moba(q, k, v, *, batch, seq, heads, hp=MOBA_HEADS_PER_STEP):
    m = batch * seq
    blk, dh = MOBA_BLOCK, MOBA_HEAD_DIM
    n_blocks = seq // blk
    km_rows = -(-n_blocks // BF16_SUBLANES) * BF16_SUBLANES
    return pl.pallas_call(
        functools.partial(_moba_body, n_blocks=n_blocks, group=MOBA_KEY_GROUP, hp=hp),
        grid=(batch, heads // hp, n_blocks),
        in_specs=[pl.BlockSpec((blk, hp * dh), lambda b, h, t: (b * n_blocks + t, h)),
                  pl.BlockSpec((blk, hp * dh),
                               lambda b, h, t: (b * n_blocks + jnp.minimum(t + 1, n_blocks - 1), h)),
                  pl.BlockSpec((seq, hp * dh), lambda b, h, t: (b, h)),
                  pl.BlockSpec((seq, hp * dh), lambda b, h, t: (b, h))],
        out_specs=pl.BlockSpec((blk, hp * dh), lambda b, h, t: (b * n_blocks + t, h)),
        out_shape=jax.ShapeDtypeStruct((m, heads * dh), BF16),
        scratch_shapes=[pltpu.VMEM((hp, seq, 2 * dh), BF16),
                        pltpu.VMEM((hp, seq, 2 * dh), BF16),
                        pltpu.VMEM((hp, km_rows, dh), BF16),
                        pltpu.VMEM((hp, km_rows, dh), BF16),
                        pltpu.VMEM((2, hp, blk, dh), BF16)],
        compiler_params=_params("parallel", "parallel", "arbitrary"),
        name="moba",
    )(q, q, k, v)


def _mix_out_body(a1_ref, a2_ref, w_ref, r_ref, o_ref, hb_ref, ss_ref):
    half = a1_ref.shape[1]
    acc = jnp.dot(a1_ref[...], w_ref[:half, :], preferred_element_type=F32)
    acc = acc + jnp.dot(a2_ref[...], w_ref[half:, :], preferred_element_type=F32)
    h = acc + r_ref[...]
    o_ref[...] = h
    _emit_norm_parts(h, hb_ref, ss_ref)


def _mix_out(a1, a2, w, res, tm=1024, tn=1024):
    m, half = a1.shape
    n = w.shape[1]
    tm, tn = min(tm, m), min(tn, n)
    return pl.pallas_call(
        _mix_out_body,
        grid=(m // tm, n // tn),
        in_specs=[pl.BlockSpec((tm, half), lambda i, j: (i, 0)),
                  pl.BlockSpec((tm, half), lambda i, j: (i, 0)),
                  pl.BlockSpec((2 * half, tn), lambda i, j: (0, j)),
                  pl.BlockSpec((tm, tn), lambda i, j: (i, j))],
        out_specs=[pl.BlockSpec((tm, tn), lambda i, j: (i, j)),
                   pl.BlockSpec((tm, tn), lambda i, j: (i, j)),
                   pl.BlockSpec((tm, LANES), lambda i, j: (i, j))],
        out_shape=[jax.ShapeDtypeStruct((m, n), F32), jax.ShapeDtypeStruct((m, n), BF16),
                   jax.ShapeDtypeStruct((m, (n // tn) * LANES), F32)],
        compiler_params=_params("parallel", "parallel"),
        name="mix_out",
    )(a1, a2, w, res)


def _cross_body(q_ref, k_ref, v_ref, o_ref, *, heads):
    d = q_ref.shape[1]
    dh = d // heads
    scale = dh ** -0.5
    for h in range(heads):
        sl = slice(h * dh, (h + 1) * dh)
        s = lax.dot_general(q_ref[:, sl], k_ref[:, sl], (((1,), (1,)), ((), ())),
                            preferred_element_type=F32) * scale
        s = s - jnp.max(s, axis=-1, keepdims=True)
        e = jnp.exp(s)
        p = e / jnp.sum(e, axis=-1, keepdims=True)
        o_ref[:, sl] = jnp.dot(p.astype(BF16), v_ref[:, sl],
                               preferred_element_type=F32).astype(o_ref.dtype)


def _cross(q, k, v, *, batch, seq, n_mem, heads, rows=512):
    m, d = q.shape
    rows = min(rows, seq)
    steps = seq // rows
    return pl.pallas_call(
        functools.partial(_cross_body, heads=heads),
        grid=(batch, steps),
        in_specs=[pl.BlockSpec((rows, d), lambda b, t: (b * steps + t, 0)),
                  pl.BlockSpec((n_mem, d), lambda b, t: (b, 0)),
                  pl.BlockSpec((n_mem, d), lambda b, t: (b, 0))],
        out_specs=pl.BlockSpec((rows, d), lambda b, t: (b * steps + t, 0)),
        out_shape=jax.ShapeDtypeStruct((m, d), BF16),
        compiler_params=_params("parallel", "parallel"),
        name="cross_attn",
    )(q, k, v)


def _layer(h, mem2, batch, seq, n_mem, norm_mix_g, w_in, w_gate_up, b_gate, gla_norm_g, w_out,
           norm_cross_g, norm_mem_g, w_cq, w_ck, w_cv, w_co, norm_mlp_g, w_up, w_down):
    m, d = h.shape
    gla_dk = GLA_HEADS * GLA_HEAD_K
    gla_dv = GLA_HEADS * GLA_HEAD_V
    moba_w = d - gla_dv
    moba_heads = moba_w // MOBA_HEAD_DIM
    o_glow = 2 * gla_dk + gla_dv
    o_gout = o_glow + GLA_GATE_RANK
    in_width = w_in.shape[1]
    w_gu = jnp.pad(w_gate_up, ((0, LANES - GLA_GATE_RANK), (0, 0)))
    w_in_t = jnp.swapaxes(w_in, 0, 1)

    w_low_t = jnp.pad(w_in_t[o_glow:o_gout], ((0, LANES - GLA_GATE_RANK), (0, 0)))
    xn, w_gla_t, g = _norm_gate(h, norm_mix_g, (w_in_t, o_glow), w_low_t, w_gu, b_gate)
    gla_qkv, w_rest_t, w_out_b = _matmul(
        xn, w_gla_t, w_t=True, out_dtype=BF16, name="proj_gla_qkv",
        sides=[_side_rows(w_in_t, o_gout, in_width - o_gout), _side_all(w_out)])
    c_mq = gla_dv
    c_mk = c_mq + moba_w
    c_mv = c_mk + moba_w
    g_out, w_cq_b = _matmul(xn, w_rest_t, w_t=True, col0=0, n=gla_dv, out_dtype=BF16, name="proj_gla_gout",
                            sides=[_side_all(w_cq, norm_cross_g)])
    o_gla = _gla(gla_qkv, g, g_out, gla_norm_g, batch=batch, seq=seq, heads=GLA_HEADS)

    cos_t, sin_t = _rotary_tables(seq)
    mq, w_ck_b = _matmul(xn, w_rest_t, w_t=True, col0=c_mq, n=moba_w, out_dtype=BF16, name="proj_moba_q",
                         rot=(cos_t, sin_t, MOBA_HEAD_DIM ** -0.5 * LOG2E), sides=[_side_all(w_ck)])
    mk, w_cv_b = _matmul(xn, w_rest_t, w_t=True, col0=c_mk, n=moba_w, out_dtype=BF16, name="proj_moba_k",
                         rot=(cos_t, sin_t, 1.0), sides=[_side_all(w_cv)])
    mv, w_co_b = _matmul(xn, w_rest_t, w_t=True, col0=c_mv, n=moba_w, out_dtype=BF16, name="proj_moba_v",
                         sides=[_side_all(w_co)])
    o_moba = _moba(mq, mk, mv, batch=batch, seq=seq, heads=moba_heads)

    h, h_b, h_ss = _mix_out(o_gla, o_moba, w_out_b, h)
    mem_n = _rmsnorm(mem2, norm_mem_g, BF16)
    cq, w_up_b = _matmul(h_b, w_cq_b, row_ss=h_ss, out_dtype=BF16, name="proj_cross_q",
                         sides=[_side_all(w_up, norm_mlp_g)])
    ck = _matmul(mem_n, w_ck_b, out_dtype=BF16, name="proj_cross_k")
    cv = _matmul(mem_n, w_cv_b, out_dtype=BF16, name="proj_cross_v")
    o_cross = _cross(cq, ck, cv, batch=batch, seq=seq, n_mem=n_mem, heads=CROSS_HEADS)
    h, h_b, h_ss = _matmul(o_cross, w_co_b, out_dtype=F32, res=h, emit_norm=True, name="proj_cross_out")

    u, w_down_b = _matmul(h_b, w_up_b, row_ss=h_ss, out_dtype=BF16, act="relu2", name="mlp_up",
                          sides=[_side_all(w_down)])
    h = _matmul(u, w_down_b, out_dtype=F32, res=h, tk=4096, name="mlp_down")
    return h


def kernel(x, mem, norm_mix_g, w_in, w_gate_up, b_gate, gla_norm_g, w_out, norm_cross_g, norm_mem_g,
           w_cq, w_ck, w_cv, w_co, norm_mlp_g, w_up, w_down, norm_final_g):
    batch, seq, d = x.shape
    n_mem = mem.shape[1]
    h = x.reshape(batch * seq, d)
    mem2 = mem.reshape(batch * n_mem, d)
    for l in range(norm_mix_g.shape[0]):
        h = _layer(h, mem2, batch, seq, n_mem, norm_mix_g[l], w_in[l], w_gate_up[l], b_gate[l],
                   gla_norm_g[l], w_out[l], norm_cross_g[l], norm_mem_g[l], w_cq[l], w_ck[l],
                   w_cv[l], w_co[l], norm_mlp_g[l], w_up[l], w_down[l])
    out = _rmsnorm(h, norm_final_g, x.dtype)
    return out.reshape(batch, seq, d)
```

```python
import functools

import jax
import jax.numpy as jnp
from jax import lax
from jax.experimental import pallas as pl
from jax.experimental.pallas import tpu as pltpu

F32 = jnp.float32
BF16 = jnp.bfloat16

EPS = 1e-6
LANES = 128
BF16_SUBLANES = 16
VMEM_LIMIT_BYTES = 62 * 1024 * 1024

GLA_HEADS = 8
GLA_HEAD_K = 128
GLA_HEAD_V = 256
GLA_GATE_RANK = 16
GLA_GATE_NORM = 16.0
GLA_CHUNK = 64
MOBA_HEAD_DIM = 128
MOBA_BLOCK = 256
MOBA_TOPK = 3
MOBA_KEY_GROUP = 2
MOBA_HEADS_PER_STEP = 4
ROPE_THETA = 500000.0
ROPE_DIM = MOBA_HEAD_DIM // 4
ROPE_HALF = ROPE_DIM // 2
CROSS_HEADS = 4
MASK_VALUE = -1e30
LOG2E = 1.4426950408889634


def _split_bf16(x, terms):
    parts = []
    for _ in range(terms - 1):
        p = x.astype(BF16)
        parts.append(p)
        x = x - p.astype(F32)
    parts.append(x.astype(BF16))
    return parts


def _params(*sem):
    return pltpu.CompilerParams(dimension_semantics=sem, vmem_limit_bytes=VMEM_LIMIT_BYTES)


def _rmsnorm_body(x_ref, g_ref, o_ref):
    x = x_ref[...].astype(F32)
    ms = jnp.mean(x * x, axis=-1, keepdims=True)
    y = x * lax.rsqrt(ms + EPS)
    o_ref[...] = (y * g_ref[...]).astype(o_ref.dtype)


def _rmsnorm(x, g, out_dtype, rows=256):
    m, d = x.shape
    rows = min(rows, m)
    return pl.pallas_call(
        _rmsnorm_body,
        grid=(m // rows,),
        in_specs=[pl.BlockSpec((rows, d), lambda i: (i, 0)),
                  pl.BlockSpec((1, d), lambda i: (0, 0))],
        out_specs=pl.BlockSpec((rows, d), lambda i: (i, 0)),
        out_shape=jax.ShapeDtypeStruct((m, d), out_dtype),
        compiler_params=_params("parallel"),
        name="rmsnorm",
    )(x, g.reshape(1, d).astype(F32))


def _norm_gate_body(x_ref, g_ref, src_ref, wl_ref, wu_ref, b_ref, o_ref, dst_ref, gate_ref):
    x = x_ref[...].astype(F32)
    ms = jnp.mean(x * x, axis=-1, keepdims=True)
    xn = (x * lax.rsqrt(ms + EPS) * g_ref[...]).astype(BF16)
    o_ref[...] = xn
    dst_ref[...] = src_ref[...].astype(BF16)
    g_low = lax.dot_general(xn, wl_ref[...].astype(BF16), (((1,), (1,)), ((), ())),
                            preferred_element_type=F32)
    g_hi, g_lo = _split_bf16(g_low, 2)
    w_hi, w_lo = _split_bf16(wu_ref[...], 2)
    z = (jnp.dot(g_hi, w_hi, preferred_element_type=F32) + jnp.dot(g_lo, w_hi, preferred_element_type=F32)
         + jnp.dot(g_hi, w_lo, preferred_element_type=F32))
    z = z + b_ref[...]
    log_sig = jnp.minimum(z, 0.0) - jnp.log1p(jnp.exp(-jnp.abs(z)))
    gate_ref[...] = log_sig * (1.0 / GLA_GATE_NORM)


def _norm_gate(x, g, side, w_low_t, w_up_pad, b, rows=512):
    m, d = x.shape
    rows = min(rows, m)
    steps = m // rows
    src, n_rows = side
    cols = src.shape[1]
    assert n_rows % steps == 0 and (n_rows // steps) % BF16_SUBLANES == 0 and src.dtype == F32
    r = n_rows // steps
    rank_pad, n_gate = w_up_pad.shape
    assert w_low_t.shape == (rank_pad, d)
    return pl.pallas_call(
        _norm_gate_body,
        grid=(steps,),
        in_specs=[pl.BlockSpec((rows, d), lambda i: (i, 0)),
                  pl.BlockSpec((1, d), lambda i: (0, 0)),
                  pl.BlockSpec((r, cols), lambda i: (i, 0)),
                  pl.BlockSpec((rank_pad, d), lambda i: (0, 0)),
                  pl.BlockSpec((rank_pad, n_gate), lambda i: (0, 0)),
                  pl.BlockSpec((1, n_gate), lambda i: (0, 0))],
        out_specs=[pl.BlockSpec((rows, d), lambda i: (i, 0)),
                   pl.BlockSpec((r, cols), lambda i: (i, 0)),
                   pl.BlockSpec((rows, n_gate), lambda i: (i, 0))],
        out_shape=[jax.ShapeDtypeStruct((m, d), BF16),
                   jax.ShapeDtypeStruct((n_rows, cols), BF16),
                   jax.ShapeDtypeStruct((m, n_gate), F32)],
        compiler_params=_params("parallel"),
        name="norm_gate",
    )(x, g.reshape(1, d).astype(F32), src, w_low_t, w_up_pad, b.reshape(1, n_gate).astype(F32))


def _emit_norm_parts(h, hb_ref, ss_ref):
    hb_ref[...] = h.astype(BF16)
    ss_ref[...] = jnp.broadcast_to(jnp.sum(h * h, axis=1, keepdims=True), ss_ref.shape)


def _mm_body(*refs, nk, act, has_res, rot_scale, side_gains, w_t, emit_norm, norm_width):
    it = iter(refs)
    a_ref, w_ref = next(it), next(it)
    r_ref = next(it) if has_res else None
    cos_ref, sin_ref = (next(it), next(it)) if rot_scale is not None else (None, None)
    ss_in_ref = next(it) if norm_width else None
    side_in = [(next(it), next(it) if gain else None) for gain in side_gains]
    o_ref = next(it)
    hb_ref, ss_ref = (next(it), next(it)) if emit_norm else (None, None)
    side_out = [next(it) for _ in side_gains]
    acc_ref = next(it) if nk > 1 and o_ref.dtype != F32 else o_ref

    for (src_ref, gain_ref), dst_ref in zip(side_in, side_out):
        src = src_ref[...]
        if gain_ref is not None:
            src = src * gain_ref[...]
        dst_ref[...] = src.astype(BF16)

    def product(cols=None):
        w = w_ref[...] if cols is None else (w_ref[cols, :] if w_t else w_ref[:, cols])
        if w_t:
            return lax.dot_general(a_ref[...], w, (((1,), (1,)), ((), ())), preferred_element_type=F32)
        return jnp.dot(a_ref[...], w, preferred_element_type=F32)

    def finish(acc):
        if norm_width:
            ms = jnp.sum(ss_in_ref[...], axis=1, keepdims=True) * (1.0 / (LANES * norm_width))
            acc = acc * lax.rsqrt(ms + EPS)
        if act == "relu2":
            acc = jnp.square(jnp.maximum(acc, 0.0))
        if has_res:
            acc = acc + r_ref[...]
        o_ref[...] = acc.astype(o_ref.dtype)
        if emit_norm:
            _emit_norm_parts(acc, hb_ref, ss_ref)

    def finish_rotary():
        tm, tn = o_ref.shape
        cos = cos_ref[...]
        sin = sin_ref[...]
        lane = lax.broadcasted_iota(jnp.int32, (tm, LANES), 1)
        half = tn // 2
        for c0 in (0, half):
            acc = product(slice(c0, c0 + half))
            for c in range(half // LANES):
                xs = acc[:, c * LANES:(c + 1) * LANES]
                partner = jnp.where(lane < ROPE_HALF,
                                    pltpu.roll(xs, LANES - ROPE_HALF, 1),
                                    pltpu.roll(xs, ROPE_HALF, 1))
                y = xs * cos + partner * sin
                if rot_scale != 1.0:
                    y = y * rot_scale
                o_ref[:, c0 + c * LANES:c0 + (c + 1) * LANES] = y.astype(o_ref.dtype)

    if rot_scale is not None:
        finish_rotary()
        return
    if nk == 1:
        finish(product())
        return

    k = pl.program_id(2)

    @pl.when(k == 0)
    def _():
        acc_ref[...] = product()

    @pl.when(k > 0)
    def _():
        acc_ref[...] += product()

    @pl.when(k == nk - 1)
    def _():
        finish(acc_ref[...])


def _side_all(src, gain=None):
    return (src, 0, src.shape[0], gain)


def _side_rows(src, r0, nr):
    return (src, r0, nr, None)


def _matmul(a, w, *, out_dtype, res=None, act=None, rot=None, w_t=False, col0=0, n=None, sides=(),
            row_ss=None, emit_norm=False, tm=1024, tn=1024, tk=4096, name="matmul"):
    m, kdim = a.shape
    n_total = w.shape[0] if w_t else w.shape[1]
    n = n_total if n is None else n
    tm, tn, tk = min(tm, m), min(tn, n), min(tk, kdim)
    if rot is not None:
        tm = min(tm, rot[0].shape[0])
    nk = kdim // tk
    assert col0 % tn == 0 and n % tn == 0 and m % tm == 0 and kdim % tk == 0
    j0 = col0 // tn
    nj = n // tn
    steps = (m // tm) * nj * nk
    w_spec = (pl.BlockSpec((tn, tk), lambda i, j, k: (j0 + j, k)) if w_t
              else pl.BlockSpec((tk, tn), lambda i, j, k: (k, j0 + j)))
    in_specs = [pl.BlockSpec((tm, tk), lambda i, j, k: (i, k)), w_spec]
    args = [a, w]
    if res is not None:
        in_specs.append(pl.BlockSpec((tm, tn), lambda i, j, k: (i, j)))
        args.append(res)
    if rot is not None:
        cos_t, sin_t, rot_scale = rot
        assert nk == 1 and cos_t.shape[0] % tm == 0 and (tn // 2) % LANES == 0
        tiles_per_seq = cos_t.shape[0] // tm
        in_specs += [pl.BlockSpec((tm, LANES), lambda i, j, k: (i % tiles_per_seq, 0))] * 2
        args += [cos_t, sin_t]
    else:
        rot_scale = None
    if row_ss is not None:
        assert nk == 1 and row_ss.shape[0] == m
        in_specs.append(pl.BlockSpec((tm, row_ss.shape[1]), lambda i, j, k: (i, 0)))
        args.append(row_ss)
    step = lambda i, j, k: (i * nj + j) * nk + k
    out_specs = [pl.BlockSpec((tm, tn), lambda i, j, k: (i, j))]
    out_shape = [jax.ShapeDtypeStruct((m, n), out_dtype)]
    if emit_norm:
        assert out_dtype == F32
        out_specs += [pl.BlockSpec((tm, tn), lambda i, j, k: (i, j)),
                      pl.BlockSpec((tm, LANES), lambda i, j, k: (i, j))]
        out_shape += [jax.ShapeDtypeStruct((m, n), BF16), jax.ShapeDtypeStruct((m, nj * LANES), F32)]
    side_gains = []
    for src, r0, nr, gain in sides:
        assert src.dtype == F32 and src.ndim == 2
        cols = src.shape[1]
        assert r0 % BF16_SUBLANES == 0 and nr % steps == 0 and (nr // steps) % BF16_SUBLANES == 0
        r = nr // steps
        if r0 % r == 0:
            in_specs.append(pl.BlockSpec((r, cols), lambda i, j, k, b=r0 // r: (b + step(i, j, k), 0)))
        else:
            in_specs.append(pl.BlockSpec((pl.Element(r), pl.Element(cols)),
                                         lambda i, j, k, r0=r0, r=r: (
                                             pl.multiple_of(r0 + step(i, j, k) * r, BF16_SUBLANES), 0)))
        args.append(src)
        if gain is not None:
            assert r0 == 0 and gain.shape == (nr,)
            in_specs.append(pl.BlockSpec((r, 1), lambda i, j, k: (step(i, j, k), 0)))
            args.append(gain.reshape(nr, 1).astype(F32))
        out_specs.append(pl.BlockSpec((r, cols), lambda i, j, k: (step(i, j, k), 0)))
        out_shape.append(jax.ShapeDtypeStruct((nr, cols), BF16))
        side_gains.append(gain is not None)
    scratch = [pltpu.VMEM((tm, tn), F32)] if nk > 1 and out_dtype != F32 else []
    outs = pl.pallas_call(
        functools.partial(_mm_body, nk=nk, act=act, has_res=res is not None, rot_scale=rot_scale,
                          side_gains=tuple(side_gains), w_t=w_t, emit_norm=emit_norm,
                          norm_width=kdim if row_ss is not None else 0),
        grid=(m // tm, nj, nk),
        in_specs=in_specs,
        out_specs=out_specs,
        out_shape=out_shape,
        scratch_shapes=scratch,
        compiler_params=_params("parallel", "parallel", "arbitrary"),
        name=name,
    )(*args)
    return outs[0] if len(outs) == 1 else tuple(outs)


def _rotary_tables(seq):
    inv_freq = ROPE_THETA ** (-jnp.arange(0, ROPE_DIM, 2, dtype=F32) / ROPE_DIM)
    ang = jnp.arange(seq).astype(F32)[:, None] * inv_freq[None, :]
    cos, sin = jnp.cos(ang), jnp.sin(ang)
    pad = MOBA_HEAD_DIM - ROPE_DIM
    cos_t = jnp.concatenate([cos, cos, jnp.ones((seq, pad), F32)], axis=-1)
    sin_t = jnp.concatenate([-sin, sin, jnp.zeros((seq, pad), F32)], axis=-1)
    return cos_t, sin_t


def _gla_body(q_ref, k_ref, v_ref, g_ref, go_ref, gn_ref, o_ref, st_ref, *, chunks):
    c_len = GLA_CHUNK

    @pl.when(pl.program_id(2) == 0)
    def _():
        st_ref[...] = jnp.zeros_like(st_ref)

    row = lax.broadcasted_iota(jnp.int32, (c_len, c_len), 0)
    col = lax.broadcasted_iota(jnp.int32, (c_len, c_len), 1)
    causal = col <= row
    tri = causal.astype(BF16)
    q_scale = GLA_HEAD_K ** -0.5
    gn = gn_ref[...]

    nt = (((1,), (1,)), ((), ()))
    tn = (((0,), (0,)), ((), ()))

    dk = g_ref.shape[1]
    g_all = jnp.concatenate([g_ref[pl.ds(c * c_len, c_len), :] for c in range(chunks)], axis=1)
    b_all = sum(jnp.dot(tri, part, preferred_element_type=F32) for part in _split_bf16(g_all, 3))
    b_last_all = b_all[c_len - 1:c_len, :]
    e_pos = jnp.exp(b_all)
    e_neg = jnp.exp(-b_all)
    e_end = jnp.exp(b_last_all - b_all)
    decay_all = jnp.exp(b_last_all)

    q_decs, o_intras, d_states, decays = [], [], [], []
    for c in range(chunks):
        sl = pl.ds(c * c_len, c_len)
        lanes = slice(c * dk, (c + 1) * dk)
        q = q_ref[sl, :].astype(F32) * q_scale
        k = k_ref[sl, :].astype(F32)
        v = v_ref[sl, :]
        q_dec = (q * e_pos[:, lanes]).astype(BF16)
        k_inv = (k * e_neg[:, lanes]).astype(BF16)
        k_end = (k * e_end[:, lanes]).astype(BF16)
        a = lax.dot_general(q_dec, k_inv, nt, preferred_element_type=F32)
        a = jnp.where(causal, a, 0.0).astype(BF16)
        q_decs.append(q_dec)
        o_intras.append(jnp.dot(a, v, preferred_element_type=F32))
        d_states.append(lax.dot_general(v, k_end, tn, preferred_element_type=F32))
        decays.append(decay_all[:, lanes])

    st = st_ref[...]
    states = []
    for c in range(chunks):
        states.append(st.astype(BF16))
        st = st * decays[c] + d_states[c]
    st_ref[...] = st

    for c in range(chunks):
        sl = pl.ds(c * c_len, c_len)
        o = o_intras[c] + lax.dot_general(q_decs[c], states[c], nt, preferred_element_type=F32)
        ms = jnp.mean(o * o, axis=-1, keepdims=True)
        y = o * lax.rsqrt(ms + EPS) * gn
        go = go_ref[sl, :].astype(F32)
        y = y * (go * jax.nn.sigmoid(go))
        o_ref[sl, :] = y.astype(o_ref.dtype)


def _gla(qkv, g, g_out, gn, *, batch, seq, heads, rows=1024):
    m = batch * seq
    rows = min(rows, seq)
    steps = seq // rows
    hk, hv = GLA_HEAD_K, GLA_HEAD_V
    k_off = heads
    v_off = heads * 2 * hk // hv
    row_map = lambda b, h, t: b * steps + t
    return pl.pallas_call(
        functools.partial(_gla_body, chunks=rows // GLA_CHUNK),
        grid=(batch, heads, steps),
        in_specs=[pl.BlockSpec((rows, hk), lambda b, h, t: (row_map(b, h, t), h)),
                  pl.BlockSpec((rows, hk), lambda b, h, t: (row_map(b, h, t), k_off + h)),
                  pl.BlockSpec((rows, hv), lambda b, h, t: (row_map(b, h, t), v_off + h)),
                  pl.BlockSpec((rows, hk), lambda b, h, t: (row_map(b, h, t), h)),
                  pl.BlockSpec((rows, hv), lambda b, h, t: (row_map(b, h, t), h)),
                  pl.BlockSpec((1, hv), lambda b, h, t: (0, 0))],
        out_specs=pl.BlockSpec((rows, hv), lambda b, h, t: (row_map(b, h, t), h)),
        out_shape=jax.ShapeDtypeStruct((m, heads * hv), BF16),
        scratch_shapes=[pltpu.VMEM((hv, hk), F32)],
        compiler_params=_params("parallel", "parallel", "arbitrary"),
        name="gla",
    )(qkv, qkv, qkv, g, g_out, gn.reshape(1, hv).astype(F32))


def _moba_body(q_ref, qn_ref, k_ref, v_ref, o_ref, kaug_ref, vaug_ref, kmh_ref, kml_ref, bias_ref,
               *, n_blocks, group, hp):
    blk = MOBA_BLOCK
    dh = MOBA_HEAD_DIM
    seq = n_blocks * blk
    km_rows = kmh_ref.shape[1]
    qb = pl.program_id(2)
    nt = (((1,), (1,)), ((), ()))

    @pl.when(qb == 0)
    def _():
        r = lax.broadcasted_iota(jnp.int32, (seq, dh), 0)
        l = lax.broadcasted_iota(jnp.int32, (seq, dh), 1)
        one_hot = ((l * blk <= r) & (r < (l + 1) * blk)).astype(BF16)
        ones = jnp.ones((seq, dh), BF16)
        j = lax.broadcasted_iota(jnp.int32, (km_rows, seq), 0)
        s = lax.broadcasted_iota(jnp.int32, (km_rows, seq), 1)
        avg = jnp.where((j * blk <= s) & (s < (j + 1) * blk), 1.0 / blk, 0.0).astype(BF16)
        for h in range(hp):
            k = k_ref[:, h * dh:(h + 1) * dh]
            kaug_ref[h, :, :dh] = k
            kaug_ref[h, :, dh:] = one_hot
            vaug_ref[h, :, :dh] = v_ref[:, h * dh:(h + 1) * dh]
            vaug_ref[h, :, dh:] = ones
            km = jnp.dot(avg, k, preferred_element_type=F32)
            hi = km.astype(BF16)
            kmh_ref[h] = hi
            kml_ref[h] = (km - hi.astype(F32)).astype(BF16)

    row = lax.broadcasted_iota(jnp.int32, (km_rows, blk), 0)
    qi = lax.broadcasted_iota(jnp.int32, (blk, blk), 0)
    ki = lax.broadcasted_iota(jnp.int32, (blk, blk), 1)
    causal = ki <= qi
    start = pl.multiple_of(qb * blk, blk)
    slot = qb % 2

    def select_next(h):
        qn = qn_ref[:, h * dh:(h + 1) * dh]
        gate = (lax.dot_general(kmh_ref[h], qn, nt, preferred_element_type=F32)
                + lax.dot_general(kml_ref[h], qn, nt, preferred_element_type=F32))
        past = row < qb + 1
        gm = jnp.where(past, gate, -jnp.inf)
        rank = jnp.zeros((km_rows, blk), jnp.int32)
        for j in range(n_blocks - 1):
            gj = gm[j:j + 1, :]
            beats = (gj > gm) | ((gj == gm) & (row > j))
            rank = rank + beats.astype(jnp.int32)
        sel = past & (rank < MOBA_TOPK)
        bias_t = jnp.where(sel, 0.0, MASK_VALUE)
        bias_t = jnp.concatenate([bias_t, jnp.zeros((dh - km_rows, blk), F32)], axis=0)
        bias_ref[1 - slot, h] = bias_t.T.astype(BF16)

    def attend(nb):
        for h in range(hp):
            cols = slice(h * dh, (h + 1) * dh)
            q = q_ref[:, cols]
            s_own = lax.dot_general(q, kaug_ref[h, pl.ds(start, blk), :dh], nt, preferred_element_type=F32)
            s_own = jnp.where(causal, s_own, MASK_VALUE)
            m = jnp.max(s_own, axis=-1, keepdims=True)
            v_own = vaug_ref[h, pl.ds(start, blk), :]
            if nb == 0:
                p_own = jnp.exp2((s_own - m).astype(BF16))
                acc = jnp.dot(p_own, v_own, preferred_element_type=F32)
            else:
                q_aug = jnp.concatenate([q, bias_ref[slot, h]], axis=1)
                s = lax.dot_general(q_aug, kaug_ref[h, :nb * blk, :], nt, preferred_element_type=F32)
                m = jnp.maximum(m, jnp.max(s, axis=-1, keepdims=True))
                p_own = jnp.exp2((s_own - m).astype(BF16))
                p = jnp.exp2((s - m).astype(BF16))
                acc = (jnp.dot(p_own, v_own, preferred_element_type=F32)
                       + jnp.dot(p, vaug_ref[h, :nb * blk, :], preferred_element_type=F32))
            o_ref[:, cols] = (acc[:, :dh] / acc[:, dh:dh + 1]).astype(o_ref.dtype)
        for h in range(hp):
            select_next(h)

    n_cls = -(-(n_blocks - 1) // group)
    cls = (qb + (group - 1)) // group
    for c in range(n_cls + 1):
        pl.when(cls == c)(functools.partial(attend, min(c * group, n_blocks)))


def _moba(q, k, v, *, batch, seq, heads, hp=MOBA_HEADS_PER_STEP):
    m = batch * seq
    blk, dh = MOBA_BLOCK, MOBA_HEAD_DIM
    n_blocks = seq // blk
    km_rows = -(-n_blocks // BF16_SUBLANES) * BF16_SUBLANES
    return pl.pallas_call(
        functools.partial(_moba_body, n_blocks=n_blocks, group=MOBA_KEY_GROUP, hp=hp),
        grid=(batch, heads // hp, n_blocks),
        in_specs=[pl.BlockSpec((blk, hp * dh), lambda b, h, t: (b * n_blocks + t, h)),
                  pl.BlockSpec((blk, hp * dh),
                               lambda b, h, t: (b * n_blocks + jnp.minimum(t + 1, n_blocks - 1), h)),
                  pl.BlockSpec((seq, hp * dh), lambda b, h, t: (b, h)),
                  pl.BlockSpec((seq, hp * dh), lambda b, h, t: (b, h))],
        out_specs=pl.BlockSpec((blk, hp * dh), lambda b, h, t: (b * n_blocks + t, h)),
        out_shape=jax.ShapeDtypeStruct((m, heads * dh), BF16),
        scratch_shapes=[pltpu.VMEM((hp, seq, 2 * dh), BF16),
                        pltpu.VMEM((hp, seq, 2 * dh), BF16),
                        pltpu.VMEM((hp, km_rows, dh), BF16),
                        pltpu.VMEM((hp, km_rows, dh), BF16),
                        pltpu.VMEM((2, hp, blk, dh), BF16)],
        compiler_params=_params("parallel", "parallel", "arbitrary"),
        name="moba",
    )(q, q, k, v)


def _mix_out_body(a1_ref, a2_ref, w_ref, r_ref, o_ref, hb_ref, ss_ref):
    half = a1_ref.shape[1]
    acc = jnp.dot(a1_ref[...], w_ref[:half, :], preferred_element_type=F32)
    acc = acc + jnp.dot(a2_ref[...], w_ref[half:, :], preferred_element_type=F32)
    h = acc + r_ref[...]
    o_ref[...] = h
    _emit_norm_parts(h, hb_ref, ss_ref)


def _mix_out(a1, a2, w, res, tm=1024, tn=1024):
    m, half = a1.shape
    n = w.shape[1]
    tm, tn = min(tm, m), min(tn, n)
    return pl.pallas_call(
        _mix_out_body,
        grid=(m // tm, n // tn),
        in_specs=[pl.BlockSpec((tm, half), lambda i, j: (i, 0)),
                  pl.BlockSpec((tm, half), lambda i, j: (i, 0)),
                  pl.BlockSpec((2 * half, tn), lambda i, j: (0, j)),
                  pl.BlockSpec((tm, tn), lambda i, j: (i, j))],
        out_specs=[pl.BlockSpec((tm, tn), lambda i, j: (i, j)),
                   pl.BlockSpec((tm, tn), lambda i, j: (i, j)),
                   pl.BlockSpec((tm, LANES), lambda i, j: (i, j))],
        out_shape=[jax.ShapeDtypeStruct((m, n), F32), jax.ShapeDtypeStruct((m, n), BF16),
                   jax.ShapeDtypeStruct((m, (n // tn) * LANES), F32)],
        compiler_params=_params("parallel", "parallel"),
        name="mix_out",
    )(a1, a2, w, res)


def _cross_body(q_ref, k_ref, v_ref, o_ref, *, heads, chunk):
    rows, d = q_ref.shape
    dh = d // heads
    scale = dh ** -0.5
    for r0 in range(0, rows, chunk):
        rs = pl.ds(r0, chunk)
        for h in range(heads):
            sl = slice(h * dh, (h + 1) * dh)
            s = lax.dot_general(q_ref[rs, sl], k_ref[:, sl], (((1,), (1,)), ((), ())),
                                preferred_element_type=F32) * scale
            s = s - jnp.max(s, axis=-1, keepdims=True)
            e = jnp.exp(s)
            p = e / jnp.sum(e, axis=-1, keepdims=True)
            o_ref[rs, sl] = jnp.dot(p.astype(BF16), v_ref[:, sl],
                                    preferred_element_type=F32).astype(o_ref.dtype)


def _cross(q, k, v, *, batch, seq, n_mem, heads, rows=1024, chunk=512):
    m, d = q.shape
    rows = min(rows, seq)
    chunk = min(chunk, rows)
    steps = seq // rows
    return pl.pallas_call(
        functools.partial(_cross_body, heads=heads, chunk=chunk),
        grid=(batch, steps),
        in_specs=[pl.BlockSpec((rows, d), lambda b, t: (b * steps + t, 0)),
                  pl.BlockSpec((n_mem, d), lambda b, t: (b, 0)),
                  pl.BlockSpec((n_mem, d), lambda b, t: (b, 0))],
        out_specs=pl.BlockSpec((rows, d), lambda b, t: (b * steps + t, 0)),
        out_shape=jax.ShapeDtypeStruct((m, d), BF16),
        compiler_params=_params("parallel", "parallel"),
        name="cross_attn",
    )(q, k, v)


def _layer(h, mem2, batch, seq, n_mem, norm_mix_g, w_in, w_gate_up, b_gate, gla_norm_g, w_out,
           norm_cross_g, norm_mem_g, w_cq, w_ck, w_cv, w_co, norm_mlp_g, w_up, w_down):
    m, d = h.shape
    gla_dk = GLA_HEADS * GLA_HEAD_K
    gla_dv = GLA_HEADS * GLA_HEAD_V
    moba_w = d - gla_dv
    moba_heads = moba_w // MOBA_HEAD_DIM
    o_glow = 2 * gla_dk + gla_dv
    o_gout = o_glow + GLA_GATE_RANK
    in_width = w_in.shape[1]
    w_gu = jnp.pad(w_gate_up, ((0, LANES - GLA_GATE_RANK), (0, 0)))
    w_in_t = jnp.swapaxes(w_in, 0, 1)

    w_low_t = jnp.pad(w_in_t[o_glow:o_gout], ((0, LANES - GLA_GATE_RANK), (0, 0)))
    xn, w_gla_t, g = _norm_gate(h, norm_mix_g, (w_in_t, o_glow), w_low_t, w_gu, b_gate)
    gla_qkv, w_rest_t, w_out_b = _matmul(
        xn, w_gla_t, w_t=True, out_dtype=BF16, name="proj_gla_qkv",
        sides=[_side_rows(w_in_t, o_gout, in_width - o_gout), _side_all(w_out)])
    c_mq = gla_dv
    c_mk = c_mq + moba_w
    c_mv = c_mk + moba_w
    g_out, w_cq_b = _matmul(xn, w_rest_t, w_t=True, col0=0, n=gla_dv, out_dtype=BF16, name="proj_gla_gout",
                            sides=[_side_all(w_cq, norm_cross_g)])
    o_gla = _gla(gla_qkv, g, g_out, gla_norm_g, batch=batch, seq=seq, heads=GLA_HEADS)

    cos_t, sin_t = _rotary_tables(seq)
    mq, w_ck_b = _matmul(xn, w_rest_t, w_t=True, col0=c_mq, n=moba_w, out_dtype=BF16, name="proj_moba_q",
                         rot=(cos_t, sin_t, MOBA_HEAD_DIM ** -0.5 * LOG2E), sides=[_side_all(w_ck)])
    mk, w_cv_b = _matmul(xn, w_rest_t, w_t=True, col0=c_mk, n=moba_w, out_dtype=BF16, name="proj_moba_k",
                         rot=(cos_t, sin_t, 1.0), sides=[_side_all(w_cv)])
    mv, w_co_b = _matmul(xn, w_rest_t, w_t=True, col0=c_mv, n=moba_w, out_dtype=BF16, name="proj_moba_v",
                         sides=[_side_all(w_co)])
    o_moba = _moba(mq, mk, mv, batch=batch, seq=seq, heads=moba_heads)

    h, h_b, h_ss = _mix_out(o_gla, o_moba, w_out_b, h)
    mem_n = _rmsnorm(mem2, norm_mem_g, BF16)
    cq, w_up_b = _matmul(h_b, w_cq_b, row_ss=h_ss, out_dtype=BF16, name="proj_cross_q",
                         sides=[_side_all(w_up, norm_mlp_g)])
    ck = _matmul(mem_n, w_ck_b, out_dtype=BF16, name="proj_cross_k")
    cv = _matmul(mem_n, w_cv_b, out_dtype=BF16, name="proj_cross_v")
    o_cross = _cross(cq, ck, cv, batch=batch, seq=seq, n_mem=n_mem, heads=CROSS_HEADS)
    h, h_b, h_ss = _matmul(o_cross, w_co_b, out_dtype=F32, res=h, emit_norm=True, name="proj_cross_out")

    u, w_down_b = _matmul(h_b, w_up_b, row_ss=h_ss, out_dtype=BF16, act="relu2", name="mlp_up",
                          sides=[_side_all(w_down)])
    h = _matmul(u, w_down_b, out_dtype=F32, res=h, tk=4096, name="mlp_down")
    return h


def kernel(x, mem, norm_mix_g, w_in, w_gate_up, b_gate, gla_norm_g, w_out, norm_cross_g, norm_mem_g,
           w_cq, w_ck, w_cv, w_co, norm_mlp_g, w_up, w_down, norm_final_g):
    batch, seq, d = x.shape
    n_mem = mem.shape[1]
    h = x.reshape(batch * seq, d)
    mem2 = mem.reshape(batch * n_mem, d)
    for l in range(norm_mix_g.shape[0]):
        h = _layer(h, mem2, batch, seq, n_mem, norm_mix_g[l], w_in[l], w_gate_up[l], b_gate[l],
                   gla_norm_g[l], w_out[l], norm_cross_g[l], norm_mem_g[l], w_cq[l], w_ck[l],
                   w_cv[l], w_co[l], norm_mlp_g[l], w_up[l], w_down[l])
    out = _rmsnorm(h, norm_final_g, x.dtype)
    return out.reshape(batch, seq, d)
```

```python
import functools

import jax
import jax.numpy as jnp
from jax import lax
from jax.experimental import pallas as pl
from jax.experimental.pallas import tpu as pltpu

F32 = jnp.float32
BF16 = jnp.bfloat16

EPS = 1e-6
LANES = 128
BF16_SUBLANES = 16
VMEM_LIMIT_BYTES = 62 * 1024 * 1024

GLA_HEADS = 8
GLA_HEAD_K = 128
GLA_HEAD_V = 256
GLA_GATE_RANK = 16
GLA_GATE_NORM = 16.0
GLA_CHUNK = 64
MOBA_HEAD_DIM = 128
MOBA_BLOCK = 256
MOBA_TOPK = 3
MOBA_KEY_GROUP = 1
MOBA_HEADS_PER_STEP = 4
ROPE_THETA = 500000.0
ROPE_DIM = MOBA_HEAD_DIM // 4
ROPE_HALF = ROPE_DIM // 2
CROSS_HEADS = 4
MASK_VALUE = -1e30
LOG2E = 1.4426950408889634


def _split_bf16(x, terms):
    parts = []
    for _ in range(terms - 1):
        p = x.astype(BF16)
        parts.append(p)
        x = x - p.astype(F32)
    parts.append(x.astype(BF16))
    return parts


def _params(*sem):
    return pltpu.CompilerParams(dimension_semantics=sem, vmem_limit_bytes=VMEM_LIMIT_BYTES)


def _rmsnorm_body(x_ref, g_ref, o_ref):
    x = x_ref[...].astype(F32)
    ms = jnp.mean(x * x, axis=-1, keepdims=True)
    y = x * lax.rsqrt(ms + EPS)
    o_ref[...] = (y * g_ref[...]).astype(o_ref.dtype)


def _rmsnorm(x, g, out_dtype, rows=256):
    m, d = x.shape
    rows = min(rows, m)
    return pl.pallas_call(
        _rmsnorm_body,
        grid=(m // rows,),
        in_specs=[pl.BlockSpec((rows, d), lambda i: (i, 0)),
                  pl.BlockSpec((1, d), lambda i: (0, 0))],
        out_specs=pl.BlockSpec((rows, d), lambda i: (i, 0)),
        out_shape=jax.ShapeDtypeStruct((m, d), out_dtype),
        compiler_params=_params("parallel"),
        name="rmsnorm",
    )(x, g.reshape(1, d).astype(F32))


def _norm_gate_body(x_ref, g_ref, src_ref, wl_ref, wu_ref, b_ref, o_ref, dst_ref, gate_ref):
    x = x_ref[...].astype(F32)
    ms = jnp.mean(x * x, axis=-1, keepdims=True)
    xn = (x * lax.rsqrt(ms + EPS) * g_ref[...]).astype(BF16)
    o_ref[...] = xn
    dst_ref[...] = src_ref[...].astype(BF16)
    g_low = lax.dot_general(xn, wl_ref[...].astype(BF16), (((1,), (1,)), ((), ())),
                            preferred_element_type=F32)
    g_hi, g_lo = _split_bf16(g_low, 2)
    w_hi, w_lo = _split_bf16(wu_ref[...], 2)
    z = (jnp.dot(g_hi, w_hi, preferred_element_type=F32) + jnp.dot(g_lo, w_hi, preferred_element_type=F32)
         + jnp.dot(g_hi, w_lo, preferred_element_type=F32))
    z = z + b_ref[...]
    log_sig = jnp.minimum(z, 0.0) - jnp.log1p(jnp.exp(-jnp.abs(z)))
    gate_ref[...] = log_sig * (1.0 / GLA_GATE_NORM)


def _norm_gate(x, g, side, w_low_t, w_up_pad, b, rows=512):
    m, d = x.shape
    rows = min(rows, m)
    steps = m // rows
    src, n_rows = side
    cols = src.shape[1]
    assert n_rows % steps == 0 and (n_rows // steps) % BF16_SUBLANES == 0 and src.dtype == F32
    r = n_rows // steps
    rank_pad, n_gate = w_up_pad.shape
    assert w_low_t.shape == (rank_pad, d)
    return pl.pallas_call(
        _norm_gate_body,
        grid=(steps,),
        in_specs=[pl.BlockSpec((rows, d), lambda i: (i, 0)),
                  pl.BlockSpec((1, d), lambda i: (0, 0)),
                  pl.BlockSpec((r, cols), lambda i: (i, 0)),
                  pl.BlockSpec((rank_pad, d), lambda i: (0, 0)),
                  pl.BlockSpec((rank_pad, n_gate), lambda i: (0, 0)),
                  pl.BlockSpec((1, n_gate), lambda i: (0, 0))],
        out_specs=[pl.BlockSpec((rows, d), lambda i: (i, 0)),
                   pl.BlockSpec((r, cols), lambda i: (i, 0)),
                   pl.BlockSpec((rows, n_gate), lambda i: (i, 0))],
        out_shape=[jax.ShapeDtypeStruct((m, d), BF16),
                   jax.ShapeDtypeStruct((n_rows, cols), BF16),
                   jax.ShapeDtypeStruct((m, n_gate), F32)],
        compiler_params=_params("parallel"),
        name="norm_gate",
    )(x, g.reshape(1, d).astype(F32), src, w_low_t, w_up_pad, b.reshape(1, n_gate).astype(F32))


def _emit_norm_parts(h, hb_ref, ss_ref):
    hb_ref[...] = h.astype(BF16)
    ss_ref[...] = jnp.broadcast_to(jnp.sum(h * h, axis=1, keepdims=True), ss_ref.shape)


def _mm_body(*refs, nk, act, has_res, rot_scale, side_gains, w_t, emit_norm, norm_width):
    it = iter(refs)
    a_ref, w_ref = next(it), next(it)
    r_ref = next(it) if has_res else None
    cos_ref, sin_ref = (next(it), next(it)) if rot_scale is not None else (None, None)
    ss_in_ref = next(it) if norm_width else None
    side_in = [(next(it), next(it) if gain else None) for gain in side_gains]
    o_ref = next(it)
    hb_ref, ss_ref = (next(it), next(it)) if emit_norm else (None, None)
    side_out = [next(it) for _ in side_gains]
    acc_ref = next(it) if nk > 1 and o_ref.dtype != F32 else o_ref

    for (src_ref, gain_ref), dst_ref in zip(side_in, side_out):
        src = src_ref[...]
        if gain_ref is not None:
            src = src * gain_ref[...]
        dst_ref[...] = src.astype(BF16)

    def product(cols=None):
        w = w_ref[...] if cols is None else (w_ref[cols, :] if w_t else w_ref[:, cols])
        if w_t:
            return lax.dot_general(a_ref[...], w, (((1,), (1,)), ((), ())), preferred_element_type=F32)
        return jnp.dot(a_ref[...], w, preferred_element_type=F32)

    def finish(acc):
        if norm_width:
            ms = jnp.sum(ss_in_ref[...], axis=1, keepdims=True) * (1.0 / (LANES * norm_width))
            acc = acc * lax.rsqrt(ms + EPS)
        if act == "relu2":
            acc = jnp.square(jnp.maximum(acc, 0.0))
        if has_res:
            acc = acc + r_ref[...]
        o_ref[...] = acc.astype(o_ref.dtype)
        if emit_norm:
            _emit_norm_parts(acc, hb_ref, ss_ref)

    def finish_rotary():
        tm, tn = o_ref.shape
        cos = cos_ref[...]
        sin = sin_ref[...]
        lane = lax.broadcasted_iota(jnp.int32, (tm, LANES), 1)
        half = tn // 2
        for c0 in (0, half):
            acc = product(slice(c0, c0 + half))
            for c in range(half // LANES):
                xs = acc[:, c * LANES:(c + 1) * LANES]
                partner = jnp.where(lane < ROPE_HALF,
                                    pltpu.roll(xs, LANES - ROPE_HALF, 1),
                                    pltpu.roll(xs, ROPE_HALF, 1))
                y = xs * cos + partner * sin
                if rot_scale != 1.0:
                    y = y * rot_scale
                o_ref[:, c0 + c * LANES:c0 + (c + 1) * LANES] = y.astype(o_ref.dtype)

    if rot_scale is not None:
        finish_rotary()
        return
    if nk == 1:
        finish(product())
        return

    k = pl.program_id(2)

    @pl.when(k == 0)
    def _():
        acc_ref[...] = product()

    @pl.when(k > 0)
    def _():
        acc_ref[...] += product()

    @pl.when(k == nk - 1)
    def _():
        finish(acc_ref[...])


def _side_all(src, gain=None):
    return (src, 0, src.shape[0], gain)


def _side_rows(src, r0, nr):
    return (src, r0, nr, None)


def _matmul(a, w, *, out_dtype, res=None, act=None, rot=None, w_t=False, col0=0, n=None, sides=(),
            row_ss=None, emit_norm=False, tm=1024, tn=1024, tk=4096, name="matmul"):
    m, kdim = a.shape
    n_total = w.shape[0] if w_t else w.shape[1]
    n = n_total if n is None else n
    tm, tn, tk = min(tm, m), min(tn, n), min(tk, kdim)
    if rot is not None:
        tm = min(tm, rot[0].shape[0])
    nk = kdim // tk
    assert col0 % tn == 0 and n % tn == 0 and m % tm == 0 and kdim % tk == 0
    j0 = col0 // tn
    nj = n // tn
    steps = (m // tm) * nj * nk
    w_spec = (pl.BlockSpec((tn, tk), lambda i, j, k: (j0 + j, k)) if w_t
              else pl.BlockSpec((tk, tn), lambda i, j, k: (k, j0 + j)))
    in_specs = [pl.BlockSpec((tm, tk), lambda i, j, k: (i, k)), w_spec]
    args = [a, w]
    if res is not None:
        in_specs.append(pl.BlockSpec((tm, tn), lambda i, j, k: (i, j)))
        args.append(res)
    if rot is not None:
        cos_t, sin_t, rot_scale = rot
        assert nk == 1 and cos_t.shape[0] % tm == 0 and (tn // 2) % LANES == 0
        tiles_per_seq = cos_t.shape[0] // tm
        in_specs += [pl.BlockSpec((tm, LANES), lambda i, j, k: (i % tiles_per_seq, 0))] * 2
        args += [cos_t, sin_t]
    else:
        rot_scale = None
    if row_ss is not None:
        assert nk == 1 and row_ss.shape[0] == m
        in_specs.append(pl.BlockSpec((tm, row_ss.shape[1]), lambda i, j, k: (i, 0)))
        args.append(row_ss)
    step = lambda i, j, k: (i * nj + j) * nk + k
    out_specs = [pl.BlockSpec((tm, tn), lambda i, j, k: (i, j))]
    out_shape = [jax.ShapeDtypeStruct((m, n), out_dtype)]
    if emit_norm:
        assert out_dtype == F32
        out_specs += [pl.BlockSpec((tm, tn), lambda i, j, k: (i, j)),
                      pl.BlockSpec((tm, LANES), lambda i, j, k: (i, j))]
        out_shape += [jax.ShapeDtypeStruct((m, n), BF16), jax.ShapeDtypeStruct((m, nj * LANES), F32)]
    side_gains = []
    for src, r0, nr, gain in sides:
        assert src.dtype == F32 and src.ndim == 2
        cols = src.shape[1]
        assert r0 % BF16_SUBLANES == 0 and nr % steps == 0 and (nr // steps) % BF16_SUBLANES == 0
        r = nr // steps
        if r0 % r == 0:
            in_specs.append(pl.BlockSpec((r, cols), lambda i, j, k, b=r0 // r: (b + step(i, j, k), 0)))
        else:
            in_specs.append(pl.BlockSpec((pl.Element(r), pl.Element(cols)),
                                         lambda i, j, k, r0=r0, r=r: (
                                             pl.multiple_of(r0 + step(i, j, k) * r, BF16_SUBLANES), 0)))
        args.append(src)
        if gain is not None:
            assert r0 == 0 and gain.shape == (nr,)
            in_specs.append(pl.BlockSpec((r, 1), lambda i, j, k: (step(i, j, k), 0)))
            args.append(gain.reshape(nr, 1).astype(F32))
        out_specs.append(pl.BlockSpec((r, cols), lambda i, j, k: (step(i, j, k), 0)))
        out_shape.append(jax.ShapeDtypeStruct((nr, cols), BF16))
        side_gains.append(gain is not None)
    scratch = [pltpu.VMEM((tm, tn), F32)] if nk > 1 and out_dtype != F32 else []
    outs = pl.pallas_call(
        functools.partial(_mm_body, nk=nk, act=act, has_res=res is not None, rot_scale=rot_scale,
                          side_gains=tuple(side_gains), w_t=w_t, emit_norm=emit_norm,
                          norm_width=kdim if row_ss is not None else 0),
        grid=(m // tm, nj, nk),
        in_specs=in_specs,
        out_specs=out_specs,
        out_shape=out_shape,
        scratch_shapes=scratch,
        compiler_params=_params("parallel", "parallel", "arbitrary"),
        name=name,
    )(*args)
    return outs[0] if len(outs) == 1 else tuple(outs)


def _rotary_tables(seq):
    inv_freq = ROPE_THETA ** (-jnp.arange(0, ROPE_DIM, 2, dtype=F32) / ROPE_DIM)
    ang = jnp.arange(seq).astype(F32)[:, None] * inv_freq[None, :]
    cos, sin = jnp.cos(ang), jnp.sin(ang)
    pad = MOBA_HEAD_DIM - ROPE_DIM
    cos_t = jnp.concatenate([cos, cos, jnp.ones((seq, pad), F32)], axis=-1)
    sin_t = jnp.concatenate([-sin, sin, jnp.zeros((seq, pad), F32)], axis=-1)
    return cos_t, sin_t


def _gla_body(q_ref, k_ref, v_ref, g_ref, go_ref, gn_ref, o_ref, st_ref, *, chunks):
    c_len = GLA_CHUNK

    @pl.when(pl.program_id(2) == 0)
    def _():
        st_ref[...] = jnp.zeros_like(st_ref)

    row = lax.broadcasted_iota(jnp.int32, (c_len, c_len), 0)
    col = lax.broadcasted_iota(jnp.int32, (c_len, c_len), 1)
    causal = col <= row
    tri = causal.astype(BF16)
    q_scale = GLA_HEAD_K ** -0.5
    gn = gn_ref[...]

    nt = (((1,), (1,)), ((), ()))
    tn = (((0,), (0,)), ((), ()))

    dk = g_ref.shape[1]
    g_all = jnp.concatenate([g_ref[pl.ds(c * c_len, c_len), :] for c in range(chunks)], axis=1)
    b_all = sum(jnp.dot(tri, part, preferred_element_type=F32) for part in _split_bf16(g_all, 3))
    b_last_all = b_all[c_len - 1:c_len, :]
    e_pos = jnp.exp(b_all)
    e_neg = jnp.exp(-b_all)
    e_end = jnp.exp(b_last_all - b_all)
    decay_all = jnp.exp(b_last_all)

    q_decs, o_intras, d_states, decays = [], [], [], []
    for c in range(chunks):
        sl = pl.ds(c * c_len, c_len)
        lanes = slice(c * dk, (c + 1) * dk)
        q = q_ref[sl, :].astype(F32) * q_scale
        k = k_ref[sl, :].astype(F32)
        v = v_ref[sl, :]
        q_dec = (q * e_pos[:, lanes]).astype(BF16)
        k_inv = (k * e_neg[:, lanes]).astype(BF16)
        k_end = (k * e_end[:, lanes]).astype(BF16)
        a = lax.dot_general(q_dec, k_inv, nt, preferred_element_type=F32)
        a = jnp.where(causal, a, 0.0).astype(BF16)
        q_decs.append(q_dec)
        o_intras.append(jnp.dot(a, v, preferred_element_type=F32))
        d_states.append(lax.dot_general(v, k_end, tn, preferred_element_type=F32))
        decays.append(decay_all[:, lanes])

    st = st_ref[...]
    states = []
    for c in range(chunks):
        states.append(st.astype(BF16))
        st = st * decays[c] + d_states[c]
    st_ref[...] = st

    for c in range(chunks):
        sl = pl.ds(c * c_len, c_len)
        o = o_intras[c] + lax.dot_general(q_decs[c], states[c], nt, preferred_element_type=F32)
        ms = jnp.mean(o * o, axis=-1, keepdims=True)
        y = o * lax.rsqrt(ms + EPS) * gn
        go = go_ref[sl, :].astype(F32)
        y = y * (go * jax.nn.sigmoid(go))
        o_ref[sl, :] = y.astype(o_ref.dtype)


def _gla(qkv, g, g_out, gn, *, batch, seq, heads, rows=1024):
    m = batch * seq
    rows = min(rows, seq)
    steps = seq // rows
    hk, hv = GLA_HEAD_K, GLA_HEAD_V
    k_off = heads
    v_off = heads * 2 * hk // hv
    row_map = lambda b, h, t: b * steps + t
    return pl.pallas_call(
        functools.partial(_gla_body, chunks=rows // GLA_CHUNK),
        grid=(batch, heads, steps),
        in_specs=[pl.BlockSpec((rows, hk), lambda b, h, t: (row_map(b, h, t), h)),
                  pl.BlockSpec((rows, hk), lambda b, h, t: (row_map(b, h, t), k_off + h)),
                  pl.BlockSpec((rows, hv), lambda b, h, t: (row_map(b, h, t), v_off + h)),
                  pl.BlockSpec((rows, hk), lambda b, h, t: (row_map(b, h, t), h)),
                  pl.BlockSpec((rows, hv), lambda b, h, t: (row_map(b, h, t), h)),
                  pl.BlockSpec((1, hv), lambda b, h, t: (0, 0))],
        out_specs=pl.BlockSpec((rows, hv), lambda b, h, t: (row_map(b, h, t), h)),
        out_shape=jax.ShapeDtypeStruct((m, heads * hv), BF16),
        scratch_shapes=[pltpu.VMEM((hv, hk), F32)],
        compiler_params=_params("parallel", "parallel", "arbitrary"),
        name="gla",
    )(qkv, qkv, qkv, g, g_out, gn.reshape(1, hv).astype(F32))


def _moba_body(q_ref, qn_ref, k_ref, v_ref, o_ref, kaug_ref, vaug_ref, kmh_ref, kml_ref, bias_ref,
               *, n_blocks, group, hp):
    blk = MOBA_BLOCK
    dh = MOBA_HEAD_DIM
    seq = n_blocks * blk
    km_rows = kmh_ref.shape[1]
    qb = pl.program_id(2)
    nt = (((1,), (1,)), ((), ()))

    @pl.when(qb == 0)
    def _():
        r = lax.broadcasted_iota(jnp.int32, (seq, dh), 0)
        l = lax.broadcasted_iota(jnp.int32, (seq, dh), 1)
        one_hot = ((l * blk <= r) & (r < (l + 1) * blk)).astype(BF16)
        ones = jnp.ones((seq, dh), BF16)
        j = lax.broadcasted_iota(jnp.int32, (km_rows, seq), 0)
        s = lax.broadcasted_iota(jnp.int32, (km_rows, seq), 1)
        avg = jnp.where((j * blk <= s) & (s < (j + 1) * blk), 1.0 / blk, 0.0).astype(BF16)
        for h in range(hp):
            k = k_ref[:, h * dh:(h + 1) * dh]
            kaug_ref[h, :, :dh] = k
            kaug_ref[h, :, dh:] = one_hot
            vaug_ref[h, :, :dh] = v_ref[:, h * dh:(h + 1) * dh]
            vaug_ref[h, :, dh:] = ones
            km = jnp.dot(avg, k, preferred_element_type=F32)
            hi = km.astype(BF16)
            kmh_ref[h] = hi
            kml_ref[h] = (km - hi.astype(F32)).astype(BF16)

    row = lax.broadcasted_iota(jnp.int32, (km_rows, blk), 0)
    qi = lax.broadcasted_iota(jnp.int32, (blk, blk), 0)
    ki = lax.broadcasted_iota(jnp.int32, (blk, blk), 1)
    causal = ki <= qi
    start = pl.multiple_of(qb * blk, blk)
    slot = qb % 2

    def select_next(h):
        qn = qn_ref[:, h * dh:(h + 1) * dh]
        gate = (lax.dot_general(kmh_ref[h], qn, nt, preferred_element_type=F32)
                + lax.dot_general(kml_ref[h], qn, nt, preferred_element_type=F32))
        past = row < qb + 1
        gm = jnp.where(past, gate, -jnp.inf)
        rank = jnp.zeros((km_rows, blk), jnp.int32)
        for j in range(n_blocks - 1):
            gj = gm[j:j + 1, :]
            beats = (gj > gm) | ((gj == gm) & (row > j))
            rank = rank + beats.astype(jnp.int32)
        sel = past & (rank < MOBA_TOPK)
        bias_t = jnp.where(sel, 0.0, MASK_VALUE)
        bias_t = jnp.concatenate([bias_t, jnp.zeros((dh - km_rows, blk), F32)], axis=0)
        bias_ref[1 - slot, h] = bias_t.T.astype(BF16)

    def attend(nb):
        for h in range(hp):
            cols = slice(h * dh, (h + 1) * dh)
            q = q_ref[:, cols]
            s_own = lax.dot_general(q, kaug_ref[h, pl.ds(start, blk), :dh], nt, preferred_element_type=F32)
            s_own = jnp.where(causal, s_own, MASK_VALUE)
            m = jnp.max(s_own, axis=-1, keepdims=True)
            v_own = vaug_ref[h, pl.ds(start, blk), :]
            if nb == 0:
                p_own = jnp.exp2((s_own - m).astype(BF16))
                acc = jnp.dot(p_own, v_own, preferred_element_type=F32)
            else:
                q_aug = jnp.concatenate([q, bias_ref[slot, h]], axis=1)
                s = lax.dot_general(q_aug, kaug_ref[h, :nb * blk, :], nt, preferred_element_type=F32)
                m = jnp.maximum(m, jnp.max(s, axis=-1, keepdims=True))
                p_own = jnp.exp2((s_own - m).astype(BF16))
                p = jnp.exp2((s - m).astype(BF16))
                acc = (jnp.dot(p_own, v_own, preferred_element_type=F32)
                       + jnp.dot(p, vaug_ref[h, :nb * blk, :], preferred_element_type=F32))
            o_ref[:, cols] = (acc[:, :dh] / acc[:, dh:dh + 1]).astype(o_ref.dtype)
        for h in range(hp):
            select_next(h)

    n_cls = -(-(n_blocks - 1) // group)
    cls = (qb + (group - 1)) // group
    for c in range(n_cls + 1):
        pl.when(cls == c)(functools.partial(attend, min(c * group, n_blocks)))


def _moba(q, k, v, *, batch, seq, heads, hp=MOBA_HEADS_PER_STEP):
    m = batch * seq
    blk, dh = MOBA_BLOCK, MOBA_HEAD_DIM
    n_blocks = seq // blk
    km_rows = -(-n_blocks // BF16_SUBLANES) * BF16_SUBLANES
    return pl.pallas_call(
        functools.partial(_moba_body, n_blocks=n_blocks, group=MOBA_KEY_GROUP, hp=hp),
        grid=(batch, heads // hp, n_blocks),
        in_specs=[pl.BlockSpec((blk, hp * dh), lambda b, h, t: (b * n_blocks + t, h)),
                  pl.BlockSpec((blk, hp * dh),
                               lambda b, h, t: (b * n_blocks + jnp.minimum(t + 1, n_blocks - 1), h)),
                  pl.BlockSpec((seq, hp * dh), lambda b, h, t: (b, h)),
                  pl.BlockSpec((seq, hp * dh), lambda b, h, t: (b, h))],
        out_specs=pl.BlockSpec((blk, hp * dh), lambda b, h, t: (b * n_blocks + t, h)),
        out_shape=jax.ShapeDtypeStruct((m, heads * dh), BF16),
        scratch_shapes=[pltpu.VMEM((hp, seq, 2 * dh), BF16),
                        pltpu.VMEM((hp, seq, 2 * dh), BF16),
                        pltpu.VMEM((hp, km_rows, dh), BF16),
                        pltpu.VMEM((hp, km_rows, dh), BF16),
                        pltpu.VMEM((2, hp, blk, dh), BF16)],
        compiler_params=_params("parallel", "parallel", "arbitrary"),
        name="moba",
    )(q, q, k, v)


def _mix_out_body(a1_ref, a2_ref, w_ref, r_ref, o_ref, hb_ref, ss_ref):
    half = a1_ref.shape[1]
    acc = jnp.dot(a1_ref[...], w_ref[:half, :], preferred_element_type=F32)
    acc = acc + jnp.dot(a2_ref[...], w_ref[half:, :], preferred_element_type=F32)
    h = acc + r_ref[...]
    o_ref[...] = h
    _emit_norm_parts(h, hb_ref, ss_ref)


def _mix_out(a1, a2, w, res, tm=1024, tn=1024):
    m, half = a1.shape
    n = w.shape[1]
    tm, tn = min(tm, m), min(tn, n)
    return pl.pallas_call(
        _mix_out_body,
        grid=(m // tm, n // tn),
        in_specs=[pl.BlockSpec((tm, half), lambda i, j: (i, 0)),
                  pl.BlockSpec((tm, half), lambda i, j: (i, 0)),
                  pl.BlockSpec((2 * half, tn), lambda i, j: (0, j)),
                  pl.BlockSpec((tm, tn), lambda i, j: (i, j))],
        out_specs=[pl.BlockSpec((tm, tn), lambda i, j: (i, j)),
                   pl.BlockSpec((tm, tn), lambda i, j: (i, j)),
                   pl.BlockSpec((tm, LANES), lambda i, j: (i, j))],
        out_shape=[jax.ShapeDtypeStruct((m, n), F32), jax.ShapeDtypeStruct((m, n), BF16),
                   jax.ShapeDtypeStruct((m, (n // tn) * LANES), F32)],
        compiler_params=_params("parallel", "parallel"),
        name="mix_out",
    )(a1, a2, w, res)


def _cross_body(q_ref, k_ref, v_ref, o_ref, *, heads, chunk):
    rows, d = q_ref.shape
    dh = d // heads
    scale = dh ** -0.5
    for r0 in range(0, rows, chunk):
        rs = pl.ds(r0, chunk)
        for h in range(heads):
            sl = slice(h * dh, (h + 1) * dh)
            s = lax.dot_general(q_ref[rs, sl], k_ref[:, sl], (((1,), (1,)), ((), ())),
                                preferred_element_type=F32) * scale
            s = s - jnp.max(s, axis=-1, keepdims=True)
            e = jnp.exp(s)
            p = e / jnp.sum(e, axis=-1, keepdims=True)
            o_ref[rs, sl] = jnp.dot(p.astype(BF16), v_ref[:, sl],
                                    preferred_element_type=F32).astype(o_ref.dtype)


def _cross(q, k, v, *, batch, seq, n_mem, heads, rows=1024, chunk=512):
    m, d = q.shape
    rows = min(rows, seq)
    chunk = min(chunk, rows)
    steps = seq // rows
    return pl.pallas_call(
        functools.partial(_cross_body, heads=heads, chunk=chunk),
        grid=(batch, steps),
        in_specs=[pl.BlockSpec((rows, d), lambda b, t: (b * steps + t, 0)),
                  pl.BlockSpec((n_mem, d), lambda b, t: (b, 0)),
                  pl.BlockSpec((n_mem, d), lambda b, t: (b, 0))],
        out_specs=pl.BlockSpec((rows, d), lambda b, t: (b * steps + t, 0)),
        out_shape=jax.ShapeDtypeStruct((m, d), BF16),
        compiler_params=_params("parallel", "parallel"),
        name="cross_attn",
    )(q, k, v)


def _layer(h, mem2, batch, seq, n_mem, norm_mix_g, w_in, w_gate_up, b_gate, gla_norm_g, w_out,
           norm_cross_g, norm_mem_g, w_cq, w_ck, w_cv, w_co, norm_mlp_g, w_up, w_down):
    m, d = h.shape
    gla_dk = GLA_HEADS * GLA_HEAD_K
    gla_dv = GLA_HEADS * GLA_HEAD_V
    moba_w = d - gla_dv
    moba_heads = moba_w // MOBA_HEAD_DIM
    o_glow = 2 * gla_dk + gla_dv
    o_gout = o_glow + GLA_GATE_RANK
    in_width = w_in.shape[1]
    w_gu = jnp.pad(w_gate_up, ((0, LANES - GLA_GATE_RANK), (0, 0)))
    w_in_t = jnp.swapaxes(w_in, 0, 1)

    w_low_t = jnp.pad(w_in_t[o_glow:o_gout], ((0, LANES - GLA_GATE_RANK), (0, 0)))
    xn, w_gla_t, g = _norm_gate(h, norm_mix_g, (w_in_t, o_glow), w_low_t, w_gu, b_gate)
    gla_qkv, w_rest_t, w_out_b = _matmul(
        xn, w_gla_t, w_t=True, out_dtype=BF16, name="proj_gla_qkv",
        sides=[_side_rows(w_in_t, o_gout, in_width - o_gout), _side_all(w_out)])
    c_mq = gla_dv
    c_mk = c_mq + moba_w
    c_mv = c_mk + moba_w
    g_out, w_cq_b = _matmul(xn, w_rest_t, w_t=True, col0=0, n=gla_dv, out_dtype=BF16, name="proj_gla_gout",
                            sides=[_side_all(w_cq, norm_cross_g)])
    o_gla = _gla(gla_qkv, g, g_out, gla_norm_g, batch=batch, seq=seq, heads=GLA_HEADS)

    cos_t, sin_t = _rotary_tables(seq)
    mq, w_ck_b = _matmul(xn, w_rest_t, w_t=True, col0=c_mq, n=moba_w, out_dtype=BF16, name="proj_moba_q",
                         rot=(cos_t, sin_t, MOBA_HEAD_DIM ** -0.5 * LOG2E), sides=[_side_all(w_ck)])
    mk, w_cv_b = _matmul(xn, w_rest_t, w_t=True, col0=c_mk, n=moba_w, out_dtype=BF16, name="proj_moba_k",
                         rot=(cos_t, sin_t, 1.0), sides=[_side_all(w_cv)])
    mv, w_co_b = _matmul(xn, w_rest_t, w_t=True, col0=c_mv, n=moba_w, out_dtype=BF16, name="proj_moba_v",
                         sides=[_side_all(w_co)])
    o_moba = _moba(mq, mk, mv, batch=batch, seq=seq, heads=moba_heads)

    h, h_b, h_ss = _mix_out(o_gla, o_moba, w_out_b, h)
    mem_n = _rmsnorm(mem2, norm_mem_g, BF16)
    cq, w_up_b = _matmul(h_b, w_cq_b, row_ss=h_ss, out_dtype=BF16, name="proj_cross_q",
                         sides=[_side_all(w_up, norm_mlp_g)])
    ck = _matmul(mem_n, w_ck_b, out_dtype=BF16, name="proj_cross_k")
    cv = _matmul(mem_n, w_cv_b, out_dtype=BF16, name="proj_cross_v")
    o_cross = _cross(cq, ck, cv, batch=batch, seq=seq, n_mem=n_mem, heads=CROSS_HEADS)
    h, h_b, h_ss = _matmul(o_cross, w_co_b, out_dtype=F32, res=h, emit_norm=True, name="proj_cross_out")

    u, w_down_b = _matmul(h_b, w_up_b, row_ss=h_ss, out_dtype=BF16, act="relu2", name="mlp_up",
                          sides=[_side_all(w_down)])
    h = _matmul(u, w_down_b, out_dtype=F32, res=h, tk=4096, name="mlp_down")
    return h


def kernel(x, mem, norm_mix_g, w_in, w_gate_up, b_gate, gla_norm_g, w_out, norm_cross_g, norm_mem_g,
           w_cq, w_ck, w_cv, w_co, norm_mlp_g, w_up, w_down, norm_final_g):
    batch, seq, d = x.shape
    n_mem = mem.shape[1]
    h = x.reshape(batch * seq, d)
    mem2 = mem.reshape(batch * n_mem, d)
    for l in range(norm_mix_g.shape[0]):
        h = _layer(h, mem2, batch, seq, n_mem, norm_mix_g[l], w_in[l], w_gate_up[l], b_gate[l],
                   gla_norm_g[l], w_out[l], norm_cross_g[l], norm_mem_g[l], w_cq[l], w_ck[l],
                   w_cv[l], w_co[l], norm_mlp_g[l], w_up[l], w_down[l])
    out = _rmsnorm(h, norm_final_g, x.dtype)
    return out.reshape(batch, seq, d)
```
